```python
import math
import jax
import jax.numpy as jnp
from jax import lax
import numpy as np

D_MODEL = 1024
BATCH = 4
SEQ = 4096
DEPTH = 4
DEC_BATCH = 128
DEC_SEQ = 8
PAST_LEN = 8192
PAGE_SIZE = 128

HEAD_DIM = 64
NSA_HEADS = 8
NSA_KV_HEADS = 1
NSA_GROUP = NSA_HEADS // NSA_KV_HEADS
CMP_STRIDE = 16
CMP_BLOCK = 2 * CMP_STRIDE
CMP_HIDDEN = 64
SEL_BLOCK = 64
SEL_TOPK = 16
WINDOW = 512
MLA_HEADS = 8
MLA_NOPE = 64
MLA_ROPE = 32
MLA_V = 64
Q_LORA = 384
KV_LORA = 256
ROPE_THETA = 10000.0
D_FF = 2816
CONV_W = 3
REL_BUCKETS = 32
REL_MAX_DIST = 1024
EPS = 1e-6
Q_BLOCK = 128
MIX_WIDTH = NSA_HEADS * HEAD_DIM + MLA_HEADS * MLA_V
NSA_SCALE = HEAD_DIM ** -0.5
MLA_SCALE = (MLA_NOPE + MLA_ROPE) ** -0.5

OFF_NSA_KV = NSA_HEADS * HEAD_DIM
OFF_GATE = OFF_NSA_KV + 6 * NSA_KV_HEADS * HEAD_DIM
OFF_QLAT = OFF_GATE + 3 * NSA_HEADS
OFF_CKV = OFF_QLAT + Q_LORA
OFF_KROPE = OFF_CKV + KV_LORA
IN_COLS = OFF_KROPE + MLA_ROPE

kernel_name = 'hybrid_nsa_mla_convffn_step'


def rmsnorm(x, g):
    x32 = x.astype(jnp.float32)
    y = x32 * lax.rsqrt(jnp.mean(x32 * x32, axis=-1, keepdims=True) + EPS)
    return (y * g.astype(jnp.float32)).astype(x.dtype)


def masked_softmax(logits, mask):
    logits = jnp.where(mask, logits.astype(jnp.float32), -jnp.inf)
    m = jnp.max(logits, axis=-1, keepdims=True)
    m = jnp.where(jnp.isfinite(m), m, 0.0)
    e = jnp.exp(logits - m)
    return e / jnp.maximum(jnp.sum(e, axis=-1, keepdims=True), 1e-30)


def rel_bucket(dist):
    n = jnp.maximum(dist, 0)
    exact = REL_BUCKETS // 2
    nf = jnp.maximum(n, 1).astype(jnp.float32)
    log_ratio = jnp.log(nf / exact) / math.log(REL_MAX_DIST / exact)
    large = exact + (log_ratio * (REL_BUCKETS - exact)).astype(jnp.int32)
    return jnp.where(n < exact, n, jnp.minimum(large, REL_BUCKETS - 1))


def rel_bias_heads(rel_bias, dist):
    b = rel_bias[rel_bucket(dist)].astype(jnp.float32)
    return jnp.transpose(b.reshape(dist.shape + (NSA_KV_HEADS, NSA_GROUP)), (2, 3, 0, 1))


def rope(x, pos):
    half = MLA_ROPE // 2
    inv_freq = ROPE_THETA ** (-jnp.arange(half, dtype=jnp.float32) / half)
    ang = pos.astype(jnp.float32)[:, None] * inv_freq[None, :]
    shape = (pos.shape[0],) + (1,) * (x.ndim - 3) + (half,)
    cos, sin = jnp.cos(ang).reshape(shape), jnp.sin(ang).reshape(shape)
    x32 = x.astype(jnp.float32)
    x1, x2 = x32[..., :half], x32[..., half:]
    return jnp.concatenate([x1 * cos - x2 * sin, x1 * sin + x2 * cos], axis=-1).astype(x.dtype)


def nsa_compress(rows, pe, w1, w2):
    b, length = rows.shape[0], rows.shape[1]
    n_chunks = length // CMP_STRIDE
    chunks = rows[:, :n_chunks * CMP_STRIDE].reshape(b, n_chunks, CMP_STRIDE, NSA_KV_HEADS, HEAD_DIM)
    w1r = w1.reshape(2, CMP_STRIDE, HEAD_DIM, CMP_HIDDEN)
    pe_term = jnp.einsum('ajd,ajde->e', pe.reshape(2, CMP_STRIDE, HEAD_DIM), w1r)
    h_lo = jnp.einsum('bnjgd,jde->bnge', chunks, w1r[0])
    h_hi = jnp.einsum('bnjgd,jde->bnge', chunks, w1r[1])
    hid = jax.nn.gelu(h_lo[:, :-1] + h_hi[:, 1:] + pe_term)
    return jnp.einsum('bnge,ef->bngf', hid, w2)


def cmp_to_sel_map(n_cmp, n_sel):
    c0 = jnp.arange(n_cmp)[:, None] * CMP_STRIDE
    s0 = jnp.arange(n_sel)[None, :] * SEL_BLOCK
    return ((c0 < s0 + SEL_BLOCK) & (c0 + CMP_BLOCK > s0)).astype(jnp.float32)


def nsa_context(rows, cmp_pe, cmp_w1, cmp_w2, k_cmp_gain):
    b, length = rows.shape[0], rows.shape[1]
    kc = rmsnorm(nsa_compress(rows[:, :, 0], cmp_pe[0], cmp_w1[0], cmp_w2[0]), k_cmp_gain)
    vc = nsa_compress(rows[:, :, 1], cmp_pe[1], cmp_w1[1], cmp_w2[1])
    cmp_end = jnp.arange(kc.shape[1], dtype=jnp.int32) * CMP_STRIDE + (CMP_BLOCK - 1)
    n_sel = -(-length // SEL_BLOCK)
    sel = jnp.pad(rows[:, :, 2:4], ((0, 0), (0, n_sel * SEL_BLOCK - length), (0, 0), (0, 0), (0, 0)))
    sel = sel.reshape(b, n_sel, SEL_BLOCK, 2, NSA_KV_HEADS, HEAD_DIM)
    return kc, vc, cmp_end, sel[:, :, :, 0], sel[:, :, :, 1]


def nsa_block(q, q_pos, kw, vw, kw_pos, kc, vc, cmp_end, ks_blk, vs_blk, rel_bias):
    b = q.shape[0]
    n_cmp, n_sel_blk = kc.shape[1], ks_blk.shape[1]
    dist_c = q_pos[:, None] - cmp_end[None, :]
    lc = jnp.einsum('btgrd,bngd->bgrtn', q, kc).astype(jnp.float32) * NSA_SCALE + rel_bias_heads(rel_bias, dist_c)
    p_c = masked_softmax(lc, dist_c >= 0)
    o_c = jnp.einsum('bgrtn,bngd->btgrd', p_c.astype(vc.dtype), vc)
    imp = jnp.einsum('bgrtn,ns->bgts', p_c, cmp_to_sel_map(n_cmp, n_sel_blk))
    blk = jnp.arange(n_sel_blk)
    q_blk = (q_pos // SEL_BLOCK)[:, None]
    forced = (blk == 0) | (blk == q_blk) | (blk == q_blk - 1)
    valid = blk * SEL_BLOCK <= q_pos[:, None]
    score = jnp.where(valid, jnp.where(forced, jnp.inf, imp), -jnp.inf)
    _, idx = lax.top_k(score, min(SEL_TOPK, n_sel_blk))
    bi = jnp.arange(b)[:, None, None, None]
    gi = jnp.arange(NSA_KV_HEADS)[None, :, None, None]
    k_sel = jnp.moveaxis(ks_blk, 3, 1)[bi, gi, idx]
    v_sel = jnp.moveaxis(vs_blk, 3, 1)[bi, gi, idx]
    k_pos = idx[..., None] * SEL_BLOCK + jnp.arange(SEL_BLOCK)
    dist_s = q_pos[:, None, None] - k_pos
    rb = rel_bias.reshape(REL_BUCKETS, NSA_KV_HEADS, NSA_GROUP)
    bias_s = jnp.moveaxis(rb[rel_bucket(dist_s), gi[..., None]].astype(jnp.float32), -1, 3)
    ls = jnp.einsum('btgrd,bgtnkd->bgtrnk', q, k_sel).astype(jnp.float32) * NSA_SCALE + bias_s
    n_keys = ls.shape[-2] * SEL_BLOCK
    mask_s = (dist_s >= 0).reshape(dist_s.shape[:3] + (1, n_keys))
    p_s = masked_softmax(ls.reshape(ls.shape[:4] + (n_keys,)), mask_s).reshape(ls.shape)
    o_s = jnp.einsum('bgtrnk,bgtnkd->btgrd', p_s.astype(v_sel.dtype), v_sel)
    dist_w = q_pos[:, None] - kw_pos[None, :]
    mask_w = (dist_w >= 0) & (dist_w < WINDOW) & (kw_pos >= 0)[None, :]
    lw = jnp.einsum('btgrd,bkgd->bgrtk', q, kw).astype(jnp.float32) * NSA_SCALE + rel_bias_heads(rel_bias, dist_w)
    p_w = masked_softmax(lw, mask_w)
    o_w = jnp.einsum('bgrtk,bkgd->btgrd', p_w.astype(vw.dtype), vw)
    return o_c, o_s, o_w


def mla_dense_block(qn, qr, q_pos, kn, kr, v, k_pos):
    s = (jnp.einsum('bqhd,bkhd->bhqk', qn, kn).astype(jnp.float32)
         + jnp.einsum('bqhr,bkr->bhqk', qr, kr).astype(jnp.float32)) * MLA_SCALE
    p = masked_softmax(s, k_pos[None, :] <= q_pos[:, None])
    return jnp.einsum('bhqk,bkhd->bqhd', p.astype(v.dtype), v)


def mla_online_update(carry, qa, qr, c, kr, kscale, mask):
    m, l, acc = carry
    s_nope = jnp.einsum('bthc,bkc->bhtk', qa, c).astype(jnp.float32) * jnp.swapaxes(kscale.astype(jnp.float32), 1, 2)[:, :, None, :]
    s = (s_nope + jnp.einsum('bthr,bkr->bhtk', qr, kr).astype(jnp.float32)) * MLA_SCALE
    if mask is not None:
        s = jnp.where(mask, s, -jnp.inf)
    m_new = jnp.maximum(m, jnp.max(s, axis=-1))
    alpha = jnp.exp(m - m_new)
    p = jnp.exp(s - m_new[..., None])
    l_new = l * alpha + jnp.sum(p, axis=-1)
    acc_new = acc * alpha[..., None] + jnp.einsum('bhtk,bkc->bhtc', p, c.astype(jnp.float32))
    return (m_new, l_new, acc_new)


def project(xn, lp, pos):
    b, t, _ = xn.shape
    g_, r_ = NSA_KV_HEADS, NSA_GROUP
    cols = xn @ lp['w_in']
    nsa_norm = lp['nsa_norm']
    q_nsa = rmsnorm(cols[..., :OFF_NSA_KV].reshape(b, t, g_, r_, HEAD_DIM), nsa_norm[0])
    kv = cols[..., OFF_NSA_KV:OFF_GATE].reshape(b, t, 6, g_, HEAD_DIM)
    gates = jax.nn.sigmoid(cols[..., OFF_GATE:OFF_QLAT].astype(jnp.float32)).reshape(b, t, 3, g_, r_)
    nsa_rows = jnp.stack([kv[:, :, 0], kv[:, :, 1], rmsnorm(kv[:, :, 2], nsa_norm[2]), kv[:, :, 3]], axis=2)
    win_rows = jnp.stack([rmsnorm(kv[:, :, 4], nsa_norm[3]), kv[:, :, 5]], axis=2)
    q_lat = rmsnorm(cols[..., OFF_QLAT:OFF_CKV], lp['mla_q_lat_norm'])
    q = (q_lat @ lp['w_q_up']).reshape(b, t, MLA_HEADS, MLA_NOPE + MLA_ROPE)
    q_nope = rmsnorm(q[..., :MLA_NOPE], lp['mla_nope_norm'][0]) * lp['mla_nope_norm'][1]
    q_rope = rope(rmsnorm(q[..., MLA_NOPE:], lp['mla_rope_norm'][0]), pos)
    c = rmsnorm(cols[..., OFF_CKV:OFF_KROPE], lp['mla_kv_norm'])
    k_rope = rope(rmsnorm(cols[..., OFF_KROPE:IN_COLS], lp['mla_rope_norm'][1]), pos)
    k_nope = jnp.einsum('btc,chd->bthd', c, lp['w_uk'])
    k32 = k_nope.astype(jnp.float32)
    kscale = lax.rsqrt(jnp.mean(k32 * k32, axis=-1) + EPS).astype(xn.dtype)
    return {'q_nsa': q_nsa, 'gates': gates, 'nsa_rows': nsa_rows, 'win_rows': win_rows,
            'q_nope': q_nope, 'q_rope': q_rope, 'c': c, 'k_rope': k_rope, 'k_nope': k_nope,
            'kscale': kscale, 'mla_rows': jnp.concatenate([c, k_rope], axis=-1)}


def merge(pr, o_c, o_s, o_w, o_mla, w_o):
    b, t = o_c.shape[0], o_c.shape[1]
    g = pr['gates'][..., None]
    o_nsa = g[:, :, 0] * o_c.astype(jnp.float32) + g[:, :, 1] * o_s.astype(jnp.float32) + g[:, :, 2] * o_w.astype(jnp.float32)
    mix = jnp.concatenate([o_nsa.astype(w_o.dtype).reshape(b, t, NSA_HEADS * HEAD_DIM),
                           o_mla.astype(w_o.dtype).reshape(b, t, MLA_HEADS * MLA_V)], axis=-1)
    return mix @ w_o


def mixer_prompt(xn, lp, rel_bias):
    b, s, _ = xn.shape
    pos = jnp.arange(s, dtype=jnp.int32)
    pr = project(xn, lp, pos)
    kc, vc, cmp_end, ks_blk, vs_blk = nsa_context(pr['nsa_rows'], lp['cmp_pe'], lp['cmp_w1'], lp['cmp_w2'], lp['nsa_norm'][1])
    n_qb, n_wb = s // Q_BLOCK, WINDOW // Q_BLOCK

    def to_blocks(a):
        return jnp.moveaxis(a.reshape((b, n_qb, Q_BLOCK) + a.shape[2:]), 1, 0)

    def from_blocks(o):
        return jnp.moveaxis(o, 0, 1).reshape((b, s) + o.shape[3:])

    def band(a):
        ap = jnp.pad(a, ((0, 0), (n_wb * Q_BLOCK, 0), (0, 0), (0, 0))).reshape(b, n_qb + n_wb, Q_BLOCK, NSA_KV_HEADS, HEAD_DIM)
        return jnp.moveaxis(jnp.concatenate([ap[:, i:i + n_qb] for i in range(n_wb + 1)], axis=2), 1, 0)

    band_pos = jnp.arange(n_qb)[:, None] * Q_BLOCK - n_wb * Q_BLOCK + jnp.arange((n_wb + 1) * Q_BLOCK)[None, :]
    pos_blocks = pos.reshape(n_qb, Q_BLOCK)
    win = pr['win_rows']

    def nsa_step(args):
        q_b, p_b, kw_b, vw_b, kwp_b = args
        return nsa_block(q_b, p_b, kw_b, vw_b, kwp_b, kc, vc, cmp_end, ks_blk, vs_blk, rel_bias)

    o_c, o_s, o_w = lax.map(nsa_step, (to_blocks(pr['q_nsa']), pos_blocks, band(win[:, :, 0]), band(win[:, :, 1]), band_pos))
    kn = pr['k_nope'] * pr['kscale'][..., None]
    v = jnp.einsum('btc,chd->bthd', pr['c'], lp['w_uv'])

    def mla_step(args):
        qn_b, qr_b, p_b = args
        return mla_dense_block(qn_b, qr_b, p_b, kn, pr['k_rope'], v, pos)

    o_mla = lax.map(mla_step, (to_blocks(pr['q_nope']), to_blocks(pr['q_rope']), pos_blocks))
    out = merge(pr, from_blocks(o_c), from_blocks(o_s), from_blocks(o_w), from_blocks(o_mla), lp['w_o'])
    return out, (pr['nsa_rows'], pr['mla_rows'], pr['kscale'], win[:, -min(WINDOW, s):])


def mixer_sample(xn, lp, rel_bias, cache_nsa, cache_mla, cache_mla_kscale, win_buf, page_table, layer):
    db, t, _ = xn.shape
    n_pages = page_table.shape[1]
    past = n_pages * PAGE_SIZE
    pos = past + jnp.arange(t, dtype=jnp.int32)
    pr = project(xn, lp, pos)
    past_rows = cache_nsa[layer, page_table].reshape(db, past, 4, NSA_KV_HEADS, HEAD_DIM)
    full_rows = jnp.concatenate([past_rows, pr['nsa_rows']], axis=1)
    kc, vc, cmp_end, ks_blk, vs_blk = nsa_context(full_rows, lp['cmp_pe'], lp['cmp_w1'], lp['cmp_w2'], lp['nsa_norm'][1])
    wb = win_buf.shape[1]
    win_all = jnp.concatenate([win_buf.astype(pr['win_rows'].dtype), pr['win_rows']], axis=1)
    win_pos = past - wb + jnp.arange(wb + t)
    o_c, o_s, o_w = nsa_block(pr['q_nsa'], pos, win_all[:, :, 0], win_all[:, :, 1], win_pos, kc, vc, cmp_end, ks_blk, vs_blk, rel_bias)
    qa = jnp.einsum('bthd,chd->bthc', pr['q_nope'], lp['w_uk'])
    init = (jnp.full((db, MLA_HEADS, t), -jnp.inf, jnp.float32),
            jnp.zeros((db, MLA_HEADS, t), jnp.float32),
            jnp.zeros((db, MLA_HEADS, t, KV_LORA), jnp.float32))

    def page_step(carry, i):
        phys = page_table[:, i]
        rows = cache_mla[layer, phys]
        return mla_online_update(carry, qa, pr['q_rope'], rows[..., :KV_LORA], rows[..., KV_LORA:], cache_mla_kscale[layer, phys], None), None

    carry, _ = lax.scan(page_step, init, jnp.arange(n_pages))
    causal = pos[None, :] <= pos[:, None]
    _, l_sum, acc = mla_online_update(carry, qa, pr['q_rope'], pr['c'], pr['k_rope'], pr['kscale'], causal)
    o_mla = jnp.einsum('bhtc,chd->bthd', (acc / l_sum[..., None]).astype(xn.dtype), lp['w_uv'])
    out = merge(pr, o_c, o_s, o_w, o_mla, lp['w_o'])
    return out, (pr['nsa_rows'], pr['mla_rows'], pr['kscale'], win_all[:, -wb:])


def conv_ffn(xn, w_up, conv_w, conv_b, w_down, state):
    h = xn @ w_up
    t = h.shape[1]
    hp = jnp.concatenate([state.astype(h.dtype), h], axis=1)
    hc = conv_b + sum(hp[:, j:j + t] * conv_w[j] for j in range(CONV_W))
    u, g = hc[..., :D_FF], hc[..., D_FF:]
    return (jax.nn.silu(g) * u) @ w_down, hp[:, t:]


def setup_inputs(seed: int = 0) -> dict:
    key = jax.random.key(seed)
    k = jax.random.split(key, 28)
    f32 = jnp.float32
    n_pages = PAST_LEN // PAGE_SIZE
    n_used = DEC_BATCH * n_pages
    n_pool = n_used + (n_used + 3) // 4
    wbuf = min(WINDOW, PAST_LEN)

    def normal(kk, shape, scale):
        return jax.random.normal(kk, shape, f32) * scale

    def gain(kk, shape):
        return 1.0 + 0.02 * jax.random.normal(kk, shape, f32)

    page_table = jax.random.permutation(k[0], n_pool)[:n_used].reshape(DEC_BATCH, n_pages).astype(jnp.int32)
    return {
        'x_prompt': jax.random.normal(k[1], (BATCH, SEQ, D_MODEL), f32),
        'x_sample': jax.random.normal(k[2], (DEC_BATCH, DEC_SEQ, D_MODEL), f32),
        'cache_nsa': jax.random.normal(k[3], (DEPTH, n_pool, PAGE_SIZE, 4, NSA_KV_HEADS, HEAD_DIM), f32),
        'cache_mla': jax.random.normal(k[4], (DEPTH, n_pool, PAGE_SIZE, KV_LORA + MLA_ROPE), f32),
        'cache_mla_kscale': jax.random.uniform(k[5], (DEPTH, n_pool, PAGE_SIZE, MLA_HEADS), f32, 0.8, 1.25),
        'state_win': jax.random.normal(k[6], (DEPTH, DEC_BATCH, wbuf, 2, NSA_KV_HEADS, HEAD_DIM), f32),
        'state_conv': jax.random.normal(k[7], (DEPTH, DEC_BATCH, CONV_W - 1, 2 * D_FF), f32),
        'page_table': page_table,
        'rel_bias': normal(k[8], (REL_BUCKETS, NSA_HEADS), 0.3),
        'attn_norm': gain(k[9], (DEPTH, D_MODEL)),
        'w_in': normal(k[10], (DEPTH, D_MODEL, IN_COLS), D_MODEL ** -0.5),
        'nsa_norm': gain(k[11], (DEPTH, 4, HEAD_DIM)),
        'cmp_pe': normal(k[12], (DEPTH, 2, CMP_BLOCK, HEAD_DIM), 0.1),
        'cmp_w1': normal(k[13], (DEPTH, 2, CMP_BLOCK * HEAD_DIM, CMP_HIDDEN), (CMP_BLOCK * HEAD_DIM) ** -0.5),
        'cmp_w2': normal(k[14], (DEPTH, 2, CMP_HIDDEN, HEAD_DIM), CMP_HIDDEN ** -0.5),
        'mla_q_lat_norm': gain(k[15], (DEPTH, Q_LORA)),
        'mla_kv_norm': gain(k[16], (DEPTH, KV_LORA)),
        'w_q_up': normal(k[17], (DEPTH, Q_LORA, MLA_HEADS * (MLA_NOPE + MLA_ROPE)), Q_LORA ** -0.5),
        'w_uk': normal(k[18], (DEPTH, KV_LORA, MLA_HEADS, MLA_NOPE), KV_LORA ** -0.5),
        'w_uv': normal(k[19], (DEPTH, KV_LORA, MLA_HEADS, MLA_V), KV_LORA ** -0.5),
        'mla_nope_norm': gain(k[20], (DEPTH, 2, MLA_NOPE)),
        'mla_rope_norm': gain(k[21], (DEPTH, 2, MLA_ROPE)),
        'w_o': normal(k[22], (DEPTH, MIX_WIDTH, D_MODEL), MIX_WIDTH ** -0.5),
        'ffn_norm': gain(k[23], (DEPTH, D_MODEL)),
        'w_up': normal(k[24], (DEPTH, D_MODEL, 2 * D_FF), D_MODEL ** -0.5),
        'conv_w': normal(k[25], (DEPTH, CONV_W, 2 * D_FF), CONV_W ** -0.5),
        'conv_b': normal(k[26], (DEPTH, 2 * D_FF), 0.02),
        'w_down': normal(k[27], (DEPTH, D_FF, D_MODEL), D_FF ** -0.5),
    }


def reference(x_prompt, x_sample, cache_nsa, cache_mla, cache_mla_kscale, state_win, state_conv, page_table,
              rel_bias, attn_norm, w_in, nsa_norm, cmp_pe, cmp_w1, cmp_w2, mla_q_lat_norm, mla_kv_norm,
              w_q_up, w_uk, w_uv, mla_nope_norm, mla_rope_norm, w_o, ffn_norm, w_up, conv_w, conv_b, w_down):
    xp, xs = x_prompt, x_sample
    outs_p = [[], [], [], [], []]
    outs_s = [[], [], [], [], []]
    for layer in range(DEPTH):
        lp = {'w_in': w_in[layer], 'nsa_norm': nsa_norm[layer], 'cmp_pe': cmp_pe[layer],
              'cmp_w1': cmp_w1[layer], 'cmp_w2': cmp_w2[layer], 'mla_q_lat_norm': mla_q_lat_norm[layer],
              'mla_kv_norm': mla_kv_norm[layer], 'w_q_up': w_q_up[layer], 'w_uk': w_uk[layer],
              'w_uv': w_uv[layer], 'mla_nope_norm': mla_nope_norm[layer],
              'mla_rope_norm': mla_rope_norm[layer], 'w_o': w_o[layer]}
        a_p, st_p = mixer_prompt(rmsnorm(xp, attn_norm[layer]), lp, rel_bias)
        xp = xp + a_p
        zero_conv = jnp.zeros((xp.shape[0], CONV_W - 1, 2 * D_FF), xp.dtype)
        f_p, conv_p = conv_ffn(rmsnorm(xp, ffn_norm[layer]), w_up[layer], conv_w[layer], conv_b[layer], w_down[layer], zero_conv)
        xp = xp + f_p
        a_s, st_s = mixer_sample(rmsnorm(xs, attn_norm[layer]), lp, rel_bias, cache_nsa, cache_mla,
                                 cache_mla_kscale, state_win[layer], page_table, layer)
        xs = xs + a_s
        f_s, conv_s = conv_ffn(rmsnorm(xs, ffn_norm[layer]), w_up[layer], conv_w[layer], conv_b[layer], w_down[layer], state_conv[layer])
        xs = xs + f_s
        for lst, val in zip(outs_p, st_p + (conv_p,)):
            lst.append(val)
        for lst, val in zip(outs_s, st_s + (conv_s,)):
            lst.append(val)
    nsa_p, mla_p, ks_p, win_p, cv_p = [jnp.stack(o, axis=0) for o in outs_p]
    nsa_s, mla_s, ks_s, win_s, cv_s = [jnp.stack(o, axis=0) for o in outs_s]
    return (xp, xs, nsa_p, mla_p, ks_p, win_p, cv_p, nsa_s, mla_s, ks_s, win_s, cv_s)
```

```python
import functools
import math

import numpy as np
import jax
import jax.numpy as jnp
from jax import lax
from jax.experimental import pallas as pl
from jax.experimental.pallas import tpu as pltpu

F32 = jnp.float32
BF16 = jnp.bfloat16

D_MODEL = 1024
PAGE_SIZE = 128
HEAD_DIM = 64
NSA_HEADS = 8
CMP_STRIDE = 16
CMP_BLOCK = 32
CMP_HIDDEN = 64
SEL_BLOCK = 64
SEL_TOPK = 16
WINDOW = 512
MLA_HEADS = 8
MLA_NOPE = 64
MLA_ROPE = 32
MLA_V = 64
Q_LORA = 384
KV_LORA = 256
ROPE_THETA = 10000.0
D_FF = 2816
CONV_W = 3
REL_BUCKETS = 32
REL_MAX_DIST = 1024
EPS = 1e-6
Q_BLOCK = 128
NSA_SCALE = HEAD_DIM ** -0.5
MLA_SCALE = (MLA_NOPE + MLA_ROPE) ** -0.5

OFF_NSA_KV = NSA_HEADS * HEAD_DIM
OFF_GATE = OFF_NSA_KV + 6 * HEAD_DIM
OFF_QLAT = OFF_GATE + 3 * NSA_HEADS
OFF_CKV = OFF_QLAT + Q_LORA
OFF_KROPE = OFF_CKV + KV_LORA
IN_COLS = OFF_KROPE + MLA_ROPE

LANE = 128
NEG = -1e30
VMEM_LIMIT = 56 * 1024 * 1024

C_Q = 0
C_KV = C_Q + NSA_HEADS * LANE
C_QL = C_KV + 3 * LANE
C_CKV = C_QL + Q_LORA
C_E = C_CKV + KV_LORA
C_END = C_E + LANE

FF_CHUNK = 256
N_FF_CHUNKS = D_FF // FF_CHUNK
PAGES_PER_STEP = 16
SEL_TK = 256
MLA_TQ = 256


def _dot(a, b):
    return jnp.dot(a, b, preferred_element_type=F32)


def _dot_nt(a, b):
    return lax.dot_general(a, b, (((1,), (1,)), ((), ())), preferred_element_type=F32)


def _params(sem=None):
    return pltpu.CompilerParams(dimension_semantics=sem, vmem_limit_bytes=VMEM_LIMIT)


def _full(shape):
    n = len(shape)
    return pl.BlockSpec(shape, lambda *_: (0,) * n)


def _bucket_thresholds():
    thr = list(range(1, REL_BUCKETS // 2 + 1))
    for k in range(1, REL_BUCKETS // 2):
        d = int(2.0 ** ((3 * k + 32) / 8.0)) - 2
        while d ** 8 < 2 ** (3 * k + 32):
            d += 1
        thr.append(d)
    return thr


_THR = _bucket_thresholds()


def _bucket_np(dist):
    n = np.maximum(np.asarray(dist, np.int64), 0)
    out = np.zeros(n.shape, np.int32)
    for t in _THR:
        out += (n >= t).astype(np.int32)
    return out


def _bias_rows(rel_bias, bucket):
    t, k = bucket.shape
    return jnp.transpose(rel_bias)[:, bucket].reshape(NSA_HEADS * t, k)


def _proj_kernel(x_ref, gattn_ref, w1_ref, gq_ref, gkv_ref, gql_ref, wqup_ref, gmq_ref, gc_ref,
                 wuk_ref, wuv_ref, ge_ref, rope_ref,
                 qnsa_ref, nsarows_ref, winrows_ref, cmp16_ref, sel16_ref, win16_ref, gates_ref,
                 qmla_ref, mlarows_ref, kscale_ref, kfull_ref, v16_ref):
    tm = x_ref.shape[0]
    lane = lax.broadcasted_iota(jnp.int32, (tm, LANE), 1)
    lo = lane < HEAD_DIM
    x = x_ref[...]
    xn = x * lax.rsqrt(jnp.mean(x * x, axis=-1, keepdims=True) + EPS) * gattn_ref[...]
    xn16 = xn.astype(BF16)

    qb = _dot(xn16, w1_ref[:, C_Q:C_KV])
    for h in range(NSA_HEADS):
        b = qb[:, h * LANE:(h + 1) * LANE]
        rs = lax.rsqrt(jnp.sum(b * b, axis=-1, keepdims=True) * (1.0 / HEAD_DIM) + EPS)
        qnsa_ref[:, h * LANE:(h + 1) * LANE] = (b * rs * gq_ref[...]).astype(qnsa_ref.dtype)

    kv = _dot(xn16, w1_ref[:, C_KV:C_QL])
    b0 = kv[:, 0:LANE]
    nsarows_ref[:, 0:LANE] = b0
    cmp16_ref[...] = b0.astype(BF16)
    for j, (dst32, dst16) in enumerate(((nsarows_ref, sel16_ref), (winrows_ref, win16_ref))):
        b = kv[:, (j + 1) * LANE:(j + 2) * LANE]
        ss = jnp.sum(jnp.where(lo, b * b, 0.0), axis=-1, keepdims=True)
        rs = lax.rsqrt(ss * (1.0 / HEAD_DIM) + EPS)
        bn = b * jnp.where(lo, rs, 1.0) * gkv_ref[j:j + 1, :]
        if j == 0:
            dst32[:, LANE:2 * LANE] = bn
        else:
            dst32[...] = bn
        dst16[...] = bn.astype(BF16)

    e = _dot(xn16, w1_ref[:, C_E:C_END])
    gates_ref[...] = 1.0 / (1.0 + jnp.exp(-e))
    rope_lanes = (lane >= HEAD_DIM) & (lane < HEAD_DIM + MLA_ROPE)
    sse = jnp.sum(jnp.where(rope_lanes, e * e, 0.0), axis=-1, keepdims=True)
    en = e * lax.rsqrt(sse * (1.0 / MLA_ROPE) + EPS) * ge_ref[...]
    krblk = en * rope_ref[:, 2 * LANE:3 * LANE] + pltpu.roll(en * rope_ref[:, 3 * LANE:4 * LANE], LANE - MLA_ROPE, 1)
    mlarows_ref[:, KV_LORA:KV_LORA + MLA_ROPE] = pltpu.roll(krblk, HEAD_DIM, 1)[:, 0:MLA_ROPE]

    ql = _dot(xn16, w1_ref[:, C_QL:C_CKV])
    qln = ql * lax.rsqrt(jnp.mean(ql * ql, axis=-1, keepdims=True) + EPS) * gql_ref[...]
    qa = _dot(qln.astype(BF16), wqup_ref[...])
    ctab = rope_ref[:, 0:LANE]
    stab = rope_ref[:, LANE:2 * LANE]
    for h in range(MLA_HEADS):
        b = qa[:, h * LANE:(h + 1) * LANE]
        b2 = b * b
        ssn = jnp.sum(jnp.where(lo, b2, 0.0), axis=-1, keepdims=True)
        ssr = jnp.sum(jnp.where(rope_lanes, b2, 0.0), axis=-1, keepdims=True)
        scale = jnp.where(lo, lax.rsqrt(ssn * (1.0 / MLA_NOPE) + EPS), lax.rsqrt(ssr * (1.0 / MLA_ROPE) + EPS))
        bn = b * scale * gmq_ref[...]
        q = bn * ctab + pltpu.roll(bn * stab, LANE - MLA_ROPE, 1)
        qmla_ref[:, h * LANE:(h + 1) * LANE] = q.astype(qmla_ref.dtype)

    cb = _dot(xn16, w1_ref[:, C_CKV:C_E])
    c = cb * lax.rsqrt(jnp.mean(cb * cb, axis=-1, keepdims=True) + EPS) * gc_ref[...]
    mlarows_ref[:, 0:KV_LORA] = c
    c16 = c.astype(BF16)
    kn = _dot(c16, wuk_ref[...])
    ksacc = jnp.zeros((tm, LANE), F32)
    for h in range(MLA_HEADS):
        b = kn[:, h * LANE:(h + 1) * LANE]
        ksh = lax.rsqrt(jnp.sum(b * b, axis=-1, keepdims=True) * (1.0 / MLA_NOPE) + EPS)
        ksacc = jnp.where(lane == h, ksh, ksacc)
        kfull_ref[:, h * LANE:(h + 1) * LANE] = (b * ksh + krblk).astype(BF16)
    kscale_ref[...] = ksacc
    v16_ref[...] = _dot(c16, wuv_ref[...]).astype(BF16)


def _proj(x, lw, rope_tab, tm, qdtype):
    t = x.shape[0]
    n_rope = rope_tab.shape[0] // tm
    row = lambda w: pl.BlockSpec((tm, w), lambda i: (i, 0))
    sds = lambda w, dt: jax.ShapeDtypeStruct((t, w), dt)
    in_specs = [row(D_MODEL), _full((1, D_MODEL)), _full((D_MODEL, C_END)), _full((1, LANE)), _full((8, LANE)),
                _full((1, Q_LORA)), _full((Q_LORA, MLA_HEADS * LANE)), _full((1, LANE)), _full((1, KV_LORA)),
                _full((KV_LORA, MLA_HEADS * LANE)), _full((KV_LORA, MLA_HEADS * MLA_V)), _full((1, LANE)),
                pl.BlockSpec((tm, 4 * LANE), lambda i: (i % n_rope, 0))]
    widths = [(NSA_HEADS * LANE, qdtype), (2 * LANE, F32), (LANE, F32), (LANE, BF16), (LANE, BF16), (LANE, BF16),
              (LANE, F32), (MLA_HEADS * LANE, qdtype), (KV_LORA + MLA_ROPE, F32), (LANE, F32),
              (MLA_HEADS * LANE, BF16), (MLA_HEADS * MLA_V, BF16)]
    return pl.pallas_call(
        _proj_kernel,
        grid=(t // tm,),
        in_specs=in_specs,
        out_specs=[row(w) for w, _ in widths],
        out_shape=[sds(w, dt) for w, dt in widths],
        compiler_params=_params(("parallel",)),
        name="proj",
    )(x, lw["gattn"], lw["w1"], lw["gq"], lw["gkv"], lw["gql"], lw["wqup"], lw["gmq"], lw["gc"],
      lw["wuk"], lw["wuv"], lw["ge"], rope_tab)


def _compress_tail(h, pe_r, w2, gk, lane_lo):
    n = h.shape[0]
    pe = pe_r[0:1, 0:LANE] + pe_r[1:2, LANE:2 * LANE]
    hid = jax.nn.gelu(h[:, 0:LANE] + pltpu.roll(h[:, LANE:2 * LANE], n - 1, 0) + pe, approximate=True)
    out = _dot(hid.astype(BF16), w2)
    ss = jnp.sum(jnp.where(lane_lo, out * out, 0.0), axis=-1, keepdims=True)
    return out * jnp.where(lane_lo, lax.rsqrt(ss * (1.0 / HEAD_DIM) + EPS), 1.0) * gk


def _compress_kernel(x_ref, w1_ref, pe_ref, w2_ref, gk_ref, out_ref):
    n = x_ref.shape[0]
    lane_lo = lax.broadcasted_iota(jnp.int32, (n, LANE), 1) < HEAD_DIM
    h = _dot(x_ref[...], w1_ref[...])
    pe_r = _dot(pe_ref[...], w1_ref[...])
    out_ref[...] = _compress_tail(h, pe_r, w2_ref[...], gk_ref[...], lane_lo).astype(BF16)


def _compress(chunks, lw, n_batch):
    n = chunks.shape[0] // n_batch
    kw = CMP_STRIDE * LANE
    return pl.pallas_call(
        _compress_kernel,
        grid=(n_batch,),
        in_specs=[pl.BlockSpec((n, kw), lambda b: (b, 0)), _full((kw, 2 * LANE)), _full((8, kw)),
                  _full((LANE, LANE)), _full((1, LANE))],
        out_specs=pl.BlockSpec((n, LANE), lambda b: (b, 0)),
        out_shape=jax.ShapeDtypeStruct((chunks.shape[0], LANE), BF16),
        compiler_params=_params(("parallel",)),
        name="compress",
    )(chunks, lw["cw1"], lw["cpe"], lw["cw2"], lw["gkc"])


def _topk_mask(score, remaining, lane):
    sel = jnp.zeros(score.shape, F32)
    big = score.shape[1]
    for _ in range(SEL_TOPK):
        cur = jnp.where(remaining, score, -jnp.inf)
        m = jnp.max(cur, axis=-1, keepdims=True)
        cand = remaining & (cur == m)
        idx = jnp.min(jnp.where(cand, lane, big), axis=-1, keepdims=True)
        pick = lane == idx
        sel = jnp.where(pick, 1.0, sel)
        remaining = remaining & jnp.logical_not(pick)
    return sel


def _softmax_rows(s, valid):
    m = jnp.max(s, axis=-1, keepdims=True)
    e = jnp.where(valid, jnp.exp(s - m), 0.0)
    return e / jnp.maximum(jnp.sum(e, axis=-1, keepdims=True), 1e-30)


def _nsa_prompt_kernel(rb_ref, q_ref, kvc_ref, sel_ref, win_ref, gates_ref, tbsel_ref, tbwin_ref, map_ref,
                       mix_ref, m_ref, l_ref, acc_ref, s_ref):
    qi = pl.program_id(1)
    q0 = qi * Q_BLOCK
    n_cmp = kvc_ref.shape[0]
    rows = NSA_HEADS * Q_BLOCK
    qblk = q_ref[...]
    qs = jnp.concatenate([qblk[:, h * LANE:(h + 1) * LANE] for h in range(NSA_HEADS)], axis=0)

    kvc = kvc_ref[...]
    sc = _dot_nt(qs, kvc)
    t_i = lax.broadcasted_iota(jnp.int32, (Q_BLOCK, n_cmp), 0)
    n_i = lax.broadcasted_iota(jnp.int32, (Q_BLOCK, n_cmp), 1)
    dist_c = q0 + t_i - (n_i * CMP_STRIDE + (CMP_BLOCK - 1))
    valid_c = dist_c >= 0
    bias = [jnp.full((Q_BLOCK, n_cmp), rb_ref[0, h], F32) for h in range(NSA_HEADS)]
    for k in range(1, REL_BUCKETS):
        mk = dist_c >= _THR[k - 1]
        bias = [jnp.where(mk, rb_ref[k, h], bias[h]) for h in range(NSA_HEADS)]
    for h in range(NSA_HEADS):
        s = jnp.where(valid_c, sc[h * Q_BLOCK:(h + 1) * Q_BLOCK, :] + bias[h], NEG)
        s_ref[h * Q_BLOCK:(h + 1) * Q_BLOCK, 0:n_cmp] = _softmax_rows(s, valid_c)
    p16 = s_ref[:, 0:n_cmp].astype(BF16)
    o_c = _dot(p16, kvc)
    imp_h = _dot(p16, map_ref[...])
    imp = imp_h[0:Q_BLOCK, :]
    for h in range(1, NSA_HEADS):
        imp = imp + imp_h[h * Q_BLOCK:(h + 1) * Q_BLOCK, :]

    n_sel = sel_ref.shape[0] // SEL_BLOCK
    lane = lax.broadcasted_iota(jnp.int32, (Q_BLOCK, LANE), 1)
    qpos = q0 + lax.broadcasted_iota(jnp.int32, (Q_BLOCK, LANE), 0)
    q_blk = qpos // SEL_BLOCK
    forced = (lane == 0) | (lane == q_blk) | (lane == q_blk - 1)
    valid_b = lane * SEL_BLOCK <= qpos
    score = jnp.where(valid_b, jnp.where(forced, jnp.inf, imp), -jnp.inf)
    sel16 = _topk_mask(score, lane < n_sel, lane).astype(BF16)

    m_ref[...] = jnp.full(m_ref.shape, NEG, F32)
    l_ref[...] = jnp.zeros(l_ref.shape, F32)
    acc_ref[...] = jnp.zeros(acc_ref.shape, F32)
    blk_i = lax.broadcasted_iota(jnp.int32, (LANE, SEL_TK), 0)
    col_i = lax.broadcasted_iota(jnp.int32, (LANE, SEL_TK), 1)
    row_t = lax.broadcasted_iota(jnp.int32, (Q_BLOCK, SEL_TK), 0)
    col_t = lax.broadcasted_iota(jnp.int32, (Q_BLOCK, SEL_TK), 1)
    n_const = tbsel_ref.shape[0] - 1

    def sel_step(kt, carry):
        k0 = pl.multiple_of(kt * SEL_TK, SEL_TK)
        kv = sel_ref[pl.ds(k0, SEL_TK), :]
        s = _dot_nt(qs, kv)
        expand = jnp.where(blk_i == (k0 + col_i) // SEL_BLOCK, 1.0, 0.0).astype(BF16)
        mk = _dot(sel16, expand)
        mk = jnp.where(k0 + col_t <= q0 + row_t, mk, 0.0)
        valid = jnp.concatenate([mk] * NSA_HEADS, axis=0) > 0.5
        tb = []
        for j in range(SEL_TK // LANE):
            didx = jnp.clip((q0 - k0) // LANE - j, 0, n_const)
            tb.append(tbsel_ref[didx])
        s = jnp.where(valid, s + jnp.concatenate(tb, axis=1), NEG)
        m_old = m_ref[...]
        m_new = jnp.maximum(m_old, jnp.max(s, axis=-1, keepdims=True))
        alpha = jnp.exp(m_old - m_new)
        p = jnp.where(valid, jnp.exp(s - m_new), 0.0)
        l_ref[...] = alpha * l_ref[...] + jnp.sum(p, axis=-1, keepdims=True)
        acc_ref[...] = alpha * acc_ref[...] + _dot(p.astype(BF16), kv)
        m_ref[...] = m_new
        return carry

    lax.fori_loop(0, q0 // SEL_TK + 1, sel_step, 0)
    o_s = acc_ref[...] / jnp.maximum(l_ref[...], 1e-30)

    n_w = WINDOW + Q_BLOCK
    kvw = win_ref[pl.ds(pl.multiple_of(q0, Q_BLOCK), n_w), :]
    sw = _dot_nt(qs, kvw)
    col_w = lax.broadcasted_iota(jnp.int32, (Q_BLOCK, n_w), 1)
    pos_ok = q0 - WINDOW + col_w >= 0
    for h in range(NSA_HEADS):
        s = jnp.where(pos_ok, sw[h * Q_BLOCK:(h + 1) * Q_BLOCK, :] + tbwin_ref[h * Q_BLOCK:(h + 1) * Q_BLOCK, :], NEG)
        s_ref[h * Q_BLOCK:(h + 1) * Q_BLOCK, 0:n_w] = _softmax_rows(s, s > 0.5 * NEG)
    o_w = _dot(s_ref[:, 0:n_w].astype(BF16), kvw)

    g = gates_ref[...]
    mixed = []
    for h in range(NSA_HEADS):
        r = slice(h * Q_BLOCK, (h + 1) * Q_BLOCK)
        mixed.append(g[:, h:h + 1] * o_c[r] + g[:, NSA_HEADS + h:NSA_HEADS + h + 1] * o_s[r]
                     + g[:, 2 * NSA_HEADS + h:2 * NSA_HEADS + h + 1] * o_w[r])
    for j in range(NSA_HEADS // 2):
        slab = jnp.where(lane < HEAD_DIM, pltpu.roll(mixed[2 * j], HEAD_DIM, 1), mixed[2 * j + 1])
        mix_ref[:, j * LANE:(j + 1) * LANE] = slab.astype(BF16)


def _nsa_prompt(rel_bias, q, kvc, sel16, win16p, gates, tbsel, tbwin, cmap, n_batch, seq):
    nqb = seq // Q_BLOCK
    n_cmp = seq // CMP_STRIDE
    rows = NSA_HEADS * Q_BLOCK
    n_w = WINDOW + Q_BLOCK
    grid_spec = pltpu.PrefetchScalarGridSpec(
        num_scalar_prefetch=0,
        grid=(n_batch, nqb),
        in_specs=[pl.BlockSpec(memory_space=pltpu.SMEM),
                  pl.BlockSpec((Q_BLOCK, NSA_HEADS * LANE), lambda b, i: (b * nqb + i, 0)),
                  pl.BlockSpec((n_cmp, LANE), lambda b, i: (b, 0)),
                  pl.BlockSpec((seq, LANE), lambda b, i: (b, 0)),
                  pl.BlockSpec((None, seq + WINDOW, LANE), lambda b, i: (b, 0, 0)),
                  pl.BlockSpec((Q_BLOCK, LANE), lambda b, i: (b * nqb + i, 0)),
                  _full(tbsel.shape), _full(tbwin.shape), _full(cmap.shape)],
        out_specs=pl.BlockSpec((Q_BLOCK, NSA_HEADS * HEAD_DIM), lambda b, i: (b * nqb + i, 0)),
        scratch_shapes=[pltpu.VMEM((rows, 1), F32), pltpu.VMEM((rows, 1), F32), pltpu.VMEM((rows, LANE), F32),
                        pltpu.VMEM((rows, max(n_cmp, n_w)), F32)],
    )
    return pl.pallas_call(
        _nsa_prompt_kernel,
        grid_spec=grid_spec,
        out_shape=jax.ShapeDtypeStruct((n_batch * seq, NSA_HEADS * HEAD_DIM), BF16),
        compiler_params=_params(("parallel", "parallel")),
        name="nsa_prompt",
    )(rel_bias, q, kvc, sel16, win16p, gates, tbsel, tbwin, cmap)


def _mla_prompt_kernel(q_ref, k_ref, v_ref, o_ref):
    qi = pl.program_id(1)
    tq = q_ref.shape[0]
    lane = lax.broadcasted_iota(jnp.int32, (tq, LANE), 1)
    row_i = lax.broadcasted_iota(jnp.int32, (tq, tq), 0)
    col_i = lax.broadcasted_iota(jnp.int32, (tq, tq), 1)
    causal = col_i <= row_i
    outs = []
    for h in range(MLA_HEADS):
        qh = q_ref[:, h * LANE:(h + 1) * LANE]
        vl = (h // 2) * LANE

        def step(kt, carry, masked, qh=qh, h=h, vl=vl):
            m_old, l_old, acc = carry
            k0 = pl.multiple_of(kt * tq, tq)
            s = _dot_nt(qh, k_ref[pl.ds(k0, tq), h * LANE:(h + 1) * LANE])
            if masked:
                s = jnp.where(causal, s, NEG)
            m_new = jnp.maximum(m_old, jnp.max(s, axis=-1, keepdims=True))
            alpha = jnp.exp(m_old - m_new)
            p = jnp.exp(s - m_new)
            if masked:
                p = jnp.where(causal, p, 0.0)
            l_new = alpha * l_old + jnp.sum(p, axis=-1, keepdims=True)
            acc = alpha * acc + _dot(p.astype(BF16), v_ref[pl.ds(k0, tq), vl:vl + LANE])
            return m_new, l_new, acc

        init = (jnp.full((tq, 1), NEG, F32), jnp.zeros((tq, 1), F32), jnp.zeros((tq, LANE), F32))
        carry = lax.fori_loop(0, qi, functools.partial(step, masked=False), init)
        _, l_fin, acc = step(qi, carry, True)
        outs.append(acc / jnp.maximum(l_fin, 1e-30))
    for j in range(MLA_HEADS // 2):
        o_ref[:, j * LANE:(j + 1) * LANE] = jnp.where(lane < MLA_V, outs[2 * j], outs[2 * j + 1]).astype(BF16)


def _mla_prompt(q, kfull, v16, n_batch, seq, tq):
    nq = seq // tq
    return pl.pallas_call(
        _mla_prompt_kernel,
        grid=(n_batch, nq),
        in_specs=[pl.BlockSpec((tq, MLA_HEADS * LANE), lambda b, i: (b * nq + i, 0)),
                  pl.BlockSpec((seq, MLA_HEADS * LANE), lambda b, i: (b, 0)),
                  pl.BlockSpec((seq, MLA_HEADS * MLA_V), lambda b, i: (b, 0))],
        out_specs=pl.BlockSpec((tq, MLA_HEADS * MLA_V), lambda b, i: (b * nq + i, 0)),
        out_shape=jax.ShapeDtypeStruct((n_batch * seq, MLA_HEADS * MLA_V), BF16),
        compiler_params=_params(("parallel", "parallel")),
        name="mla_prompt",
    )(q, kfull, v16)


def _ffn_kernel(x_ref, mixa_ref, mixb_ref, wo_ref, gffn_ref, wup_ref, cw_ref, wdn_ref, *rest, period, tiles_per_seq):
    if period is None:
        y_ref, tail_ref, hn_ref, h_ref, acc_ref, bufu_ref, bufg_ref, carry_ref = rest
    else:
        a1_ref, a2_ref, y_ref, tail_ref, hn_ref, h_ref, acc_ref, bufu_ref, bufg_ref = rest
    tm = x_ref.shape[0]
    half = wo_ref.shape[0] // 2
    h = x_ref[...] + _dot(mixa_ref[...], wo_ref[0:half, :]) + _dot(mixb_ref[...], wo_ref[half:, :])
    h_ref[...] = h
    hn = h * lax.rsqrt(jnp.mean(h * h, axis=-1, keepdims=True) + EPS) * gffn_ref[...]
    hn_ref[...] = hn.astype(BF16)
    acc_ref[...] = jnp.zeros(acc_ref.shape, F32)
    if period is None:
        first = pl.program_id(0) % tiles_per_seq == 0
    else:
        tmod = lax.broadcasted_iota(jnp.int32, (tm, FF_CHUNK), 0) % period

    def chunk(c, carry):
        hn16 = hn_ref[...]
        convd = []
        for part, buf in ((0, bufu_ref), (1, bufg_ref)):
            idx = part * N_FF_CHUNKS + c
            hc = _dot(hn16, wup_ref[idx])
            buf[8:8 + tm, :] = hc
            tail_ref[idx] = hc[tm - tail_ref.shape[1]:tm, :]
            if period is None:
                prev = carry_ref[idx]
                buf[0:8, :] = jnp.where(first, 0.0, prev)
                carry_ref[idx] = hc[tm - 8:tm, :]
                h1 = buf[7:7 + tm, :]
                h2 = buf[6:6 + tm, :]
            else:
                buf[0:8, :] = jnp.zeros((8, FF_CHUNK), F32)
                h1 = jnp.where(tmod >= 1, buf[7:7 + tm, :], a1_ref[idx])
                h2 = jnp.where(tmod >= 2, buf[6:6 + tm, :], a2_ref[idx])
            cw = cw_ref[idx]
            convd.append(cw[3:4, :] + ((h2 * cw[0:1, :] + h1 * cw[1:2, :]) + hc * cw[2:3, :]))
        u, g = convd
        act = (g * (1.0 / (1.0 + jnp.exp(-g)))) * u
        acc_ref[...] += _dot(act.astype(BF16), wdn_ref[c])
        return carry

    lax.fori_loop(0, N_FF_CHUNKS, chunk, 0)
    y_ref[...] = h_ref[...] + acc_ref[...]


def _ffn(x, mixa, mixb, lw, tm, tiles_per_seq=None, period=None, a1=None, a2=None):
    t = x.shape[0]
    n_tiles = t // tm
    row = lambda w: pl.BlockSpec((tm, w), lambda i: (i, 0))
    once = pl.Buffered(1)
    const = lambda shape: pl.BlockSpec(shape, lambda i: (0,) * len(shape), pipeline_mode=once)
    in_specs = [row(D_MODEL), row(NSA_HEADS * HEAD_DIM), row(MLA_HEADS * MLA_V),
                const((D_MODEL, D_MODEL)), const((1, D_MODEL)),
                const((2 * N_FF_CHUNKS, D_MODEL, FF_CHUNK)), const((2 * N_FF_CHUNKS, 8, FF_CHUNK)),
                const((N_FF_CHUNKS, FF_CHUNK, D_MODEL))]
    args = [x, mixa, mixb, lw["wo"], lw["gffn"], lw["wup"], lw["cw"], lw["wdn"]]
    tail_rows = 8 if period is None else tm
    scratch = [pltpu.VMEM((tm, D_MODEL), BF16), pltpu.VMEM((tm, D_MODEL), F32), pltpu.VMEM((tm, D_MODEL), F32),
               pltpu.VMEM((tm + 8, FF_CHUNK), F32), pltpu.VMEM((tm + 8, FF_CHUNK), F32)]
    if period is None:
        scratch.append(pltpu.VMEM((2 * N_FF_CHUNKS, 8, FF_CHUNK), F32))
    else:
        in_specs += [pl.BlockSpec((2 * N_FF_CHUNKS, tm, FF_CHUNK), lambda i: (0, i, 0))] * 2
        args += [a1, a2]
    return pl.pallas_call(
        functools.partial(_ffn_kernel, period=period, tiles_per_seq=tiles_per_seq),
        grid=(n_tiles,),
        in_specs=in_specs,
        out_specs=[row(D_MODEL), pl.BlockSpec((None, 2 * N_FF_CHUNKS, tail_rows, FF_CHUNK), lambda i: (i, 0, 0, 0))],
        out_shape=[jax.ShapeDtypeStruct((t, D_MODEL), F32),
                   jax.ShapeDtypeStruct((n_tiles, 2 * N_FF_CHUNKS, tail_rows, FF_CHUNK), F32)],
        scratch_shapes=scratch,
        compiler_params=_params(("arbitrary",)),
        name="ffn",
    )(*args)


def _nsa_sample_kernel(pt_ref, *refs):
    del pt_ref
    pg = PAGES_PER_STEP
    cmp_pages = refs[:pg]
    sel_pages = refs[pg:2 * pg]
    (q_ref, rows_ref, winnew_ref, state_ref, gates_ref, w1_ref, pe_ref, w2_ref, gk_ref, map_ref,
     tbc_ref, tbs_ref, tbw_ref, place_ref, mix_ref, hs_ref, sel_ref, m_ref, l_ref, acc_ref) = refs[2 * pg:]
    j = pl.program_id(1)
    n_groups = pl.num_programs(1)
    past = (sel_ref.shape[0] - PAGE_SIZE)
    t_new = q_ref.shape[0]
    chunks_per_page = PAGE_SIZE // CMP_STRIDE

    xs = []
    for k in range(pg):
        xs.append(jnp.concatenate([cmp_pages[k][pl.ds(r, chunks_per_page, stride=CMP_STRIDE), :]
                                   for r in range(CMP_STRIDE)], axis=1))
        base = pl.multiple_of((j * pg + k) * PAGE_SIZE, PAGE_SIZE)
        sel_ref[pl.ds(base, PAGE_SIZE), :] = sel_pages[k][...].astype(BF16)
    x = jnp.concatenate(xs, axis=0).astype(BF16)
    hbase = pl.multiple_of(j * (pg * chunks_per_page), pg * chunks_per_page)
    hs_ref[pl.ds(hbase, pg * chunks_per_page), :] = _dot(x, w1_ref[...])

    @pl.when(j == n_groups - 1)
    def _():
        n_cmp = hs_ref.shape[0]
        lane_c = lax.broadcasted_iota(jnp.int32, (n_cmp, LANE), 1) < HEAD_DIM
        pe_r = _dot(pe_ref[...], w1_ref[...])
        kvc = _compress_tail(hs_ref[...], pe_r, w2_ref[...], gk_ref[...], lane_c).astype(BF16)
        qblk = q_ref[...]
        qs = jnp.concatenate([qblk[:, h * LANE:(h + 1) * LANE] for h in range(NSA_HEADS)], axis=0).astype(BF16)

        tbc = tbc_ref[...]
        valid_c = tbc > 0.5 * NEG
        p16 = _softmax_rows(jnp.where(valid_c, _dot_nt(qs, kvc) + tbc, NEG), valid_c).astype(BF16)
        o_c = _dot(p16, kvc)
        imp_h = _dot(p16, map_ref[...])
        imp = imp_h[0:t_new, :]
        for h in range(1, NSA_HEADS):
            imp = imp + imp_h[h * t_new:(h + 1) * t_new, :]

        n_lane = map_ref.shape[1]
        n_sel = past // SEL_BLOCK + 1
        lane = lax.broadcasted_iota(jnp.int32, (t_new, n_lane), 1)
        qpos = past + lax.broadcasted_iota(jnp.int32, (t_new, n_lane), 0)
        q_blk = qpos // SEL_BLOCK
        forced = (lane == 0) | (lane == q_blk) | (lane == q_blk - 1)
        valid_b = lane * SEL_BLOCK <= qpos
        score = jnp.where(valid_b, jnp.where(forced, jnp.inf, imp), -jnp.inf)
        sel = _topk_mask(score, lane < n_sel, lane)
        sel16 = jnp.concatenate([sel] * NSA_HEADS, axis=0).astype(BF16)

        new = rows_ref[:, LANE:2 * LANE]
        sel_ref[pl.ds(past, PAGE_SIZE), :] = jnp.concatenate(
            [new, jnp.zeros((PAGE_SIZE - t_new, LANE), F32)], axis=0).astype(BF16)
        m_ref[...] = jnp.full(m_ref.shape, NEG, F32)
        l_ref[...] = jnp.zeros(l_ref.shape, F32)
        acc_ref[...] = jnp.zeros(acc_ref.shape, F32)
        ck = 8 * PAGE_SIZE

        def sel_chunk(k0, width):
            kv = sel_ref[pl.ds(k0, width), :]
            blk_i = lax.broadcasted_iota(jnp.int32, (n_lane, width), 0)
            col_i = lax.broadcasted_iota(jnp.int32, (n_lane, width), 1)
            expand = jnp.where(blk_i == (k0 + col_i) // SEL_BLOCK, 1.0, 0.0).astype(BF16)
            tb = tbs_ref[:, pl.ds(k0, width)]
            valid = (_dot(sel16, expand) > 0.5) & (tb > 0.5 * NEG)
            s = jnp.where(valid, _dot_nt(qs, kv) + tb, NEG)
            m_old = m_ref[...]
            m_new = jnp.maximum(m_old, jnp.max(s, axis=-1, keepdims=True))
            alpha = jnp.exp(m_old - m_new)
            p = jnp.where(valid, jnp.exp(s - m_new), 0.0)
            l_ref[...] = alpha * l_ref[...] + jnp.sum(p, axis=-1, keepdims=True)
            acc_ref[...] = alpha * acc_ref[...] + _dot(p.astype(BF16), kv)
            m_ref[...] = m_new

        def sel_step(c, carry):
            sel_chunk(pl.multiple_of(c * ck, ck), ck)
            return carry

        lax.fori_loop(0, past // ck, sel_step, 0)
        sel_chunk(past, PAGE_SIZE)
        o_s = acc_ref[...] / jnp.maximum(l_ref[...], 1e-30)

        kvw = jnp.concatenate([state_ref[...], winnew_ref[...], jnp.zeros((PAGE_SIZE - t_new, LANE), F32)],
                              axis=0).astype(BF16)
        tbw = tbw_ref[...]
        valid_w = tbw > 0.5 * NEG
        pw = _softmax_rows(jnp.where(valid_w, _dot_nt(qs, kvw) + tbw, NEG), valid_w)
        o_w = _dot(pw.astype(BF16), kvw)

        g = gates_ref[...]
        out = jnp.zeros((t_new, NSA_HEADS * HEAD_DIM), F32)
        for h in range(NSA_HEADS):
            r = slice(h * t_new, (h + 1) * t_new)
            mh = (g[:, h:h + 1] * o_c[r] + g[:, NSA_HEADS + h:NSA_HEADS + h + 1] * o_s[r]
                  + g[:, 2 * NSA_HEADS + h:2 * NSA_HEADS + h + 1] * o_w[r])
            out = out + _dot(mh.astype(BF16), place_ref[h])
        mix_ref[...] = out


def _nsa_sample(pt_l, cache_pages, q, nsarows, winrows, state, gates, lw, tabs, n_batch, t_new, n_pages):
    pg = PAGES_PER_STEP
    n_groups = n_pages // pg
    past = n_pages * PAGE_SIZE
    rows = NSA_HEADS * t_new
    n_cmp = past // CMP_STRIDE

    def page_spec(k, half):
        return pl.BlockSpec((None, PAGE_SIZE, LANE), lambda b, j, pt: (pt[b, j * pg + k], 0, half))

    per_b = lambda w: pl.BlockSpec((t_new, w), lambda b, j, pt: (b, 0))
    const = lambda a: pl.BlockSpec(a.shape, lambda b, j, pt: (0,) * a.ndim)
    consts = [lw["cw1"], lw["cpe"], lw["cw2"], lw["gkc"], tabs["map_s"], tabs["tbc_s"], tabs["tbs_s"], tabs["tbw_s"],
              tabs["place"]]
    grid_spec = pltpu.PrefetchScalarGridSpec(
        num_scalar_prefetch=1,
        grid=(n_batch, n_groups),
        in_specs=[page_spec(k, 0) for k in range(pg)] + [page_spec(k, 1) for k in range(pg)]
        + [per_b(NSA_HEADS * LANE), per_b(2 * LANE), per_b(LANE),
           pl.BlockSpec((None, state.shape[1], LANE), lambda b, j, pt: (b, 0, 0)), per_b(LANE)]
        + [const(a) for a in consts],
        out_specs=per_b(NSA_HEADS * HEAD_DIM),
        scratch_shapes=[pltpu.VMEM((n_cmp, 2 * LANE), F32), pltpu.VMEM((past + PAGE_SIZE, LANE), BF16),
                        pltpu.VMEM((rows, 1), F32), pltpu.VMEM((rows, 1), F32), pltpu.VMEM((rows, LANE), F32)],
    )
    return pl.pallas_call(
        _nsa_sample_kernel,
        grid_spec=grid_spec,
        out_shape=jax.ShapeDtypeStruct((n_batch * t_new, NSA_HEADS * HEAD_DIM), F32),
        compiler_params=_params(("parallel", "arbitrary")),
        name="nsa_sample",
    )(pt_l, *([cache_pages] * (2 * pg)), q, nsarows, winrows, state, gates, *consts)


def _mla_sample_kernel(pt_ref, *refs):
    del pt_ref
    pg = PAGES_PER_STEP
    pages = refs[:pg]
    kss = refs[pg:2 * pg]
    (q_ref, new_ref, ksnew_ref, wukt_ref, wuvp_ref, o_ref, qa_ref, qr_ref, m_ref, l_ref, acc_ref) = refs[2 * pg:]
    j = pl.program_id(1)
    n_groups = pl.num_programs(1)
    t_new = q_ref.shape[0]
    rows = MLA_HEADS * t_new

    @pl.when(j == 0)
    def _():
        qblk = q_ref[...]
        lane = lax.broadcasted_iota(jnp.int32, (t_new, LANE), 1)
        rope_lanes = (lane >= MLA_NOPE) & (lane < MLA_NOPE + MLA_ROPE)
        for h in range(MLA_HEADS):
            qh = qblk[:, h * LANE:(h + 1) * LANE]
            qa_ref[h * t_new:(h + 1) * t_new, :] = _dot(qh.astype(BF16), wukt_ref[h]).astype(BF16)
            qr = pltpu.roll(jnp.where(rope_lanes, qh, 0.0), LANE - MLA_NOPE, 1)
            qr_ref[h * t_new:(h + 1) * t_new, :] = jnp.concatenate(
                [jnp.zeros((t_new, KV_LORA), F32), qr[:, 0:MLA_ROPE]], axis=1).astype(BF16)
        m_ref[...] = jnp.full(m_ref.shape, NEG, F32)
        l_ref[...] = jnp.zeros(l_ref.shape, F32)
        acc_ref[...] = jnp.zeros(acc_ref.shape, F32)

    def update(c16, ks_rows, valid):
        s = _dot_nt(qa_ref[...], c16[:, 0:KV_LORA]) * ks_rows + _dot_nt(qr_ref[...], c16)
        if valid is not None:
            s = jnp.where(valid, s, NEG)
        m_old = m_ref[...]
        m_new = jnp.maximum(m_old, jnp.max(s, axis=-1, keepdims=True))
        alpha = jnp.exp(m_old - m_new)
        p = jnp.exp(s - m_new)
        if valid is not None:
            p = jnp.where(valid, p, 0.0)
        l_ref[...] = alpha * l_ref[...] + jnp.sum(p, axis=-1, keepdims=True)
        acc_ref[...] = alpha * acc_ref[...] + _dot(p.astype(BF16), c16[:, 0:KV_LORA])
        m_ref[...] = m_new

    def head_rows(ks):
        return jnp.concatenate([jnp.broadcast_to(ks[h:h + 1, :], (t_new, ks.shape[1])) for h in range(MLA_HEADS)],
                               axis=0)

    c16 = jnp.concatenate([pages[k][...] for k in range(pg)], axis=0).astype(BF16)
    ks = jnp.concatenate([kss[k][...] for k in range(pg)], axis=1)
    update(c16, head_rows(ks), None)

    @pl.when(j == n_groups - 1)
    def _():
        new16 = new_ref[...].astype(BF16)
        r_t = lax.broadcasted_iota(jnp.int32, (rows, t_new), 0) % t_new
        c_t = lax.broadcasted_iota(jnp.int32, (rows, t_new), 1)
        update(new16, head_rows(ksnew_ref[...]), c_t <= r_t)
        lat16 = (acc_ref[...] / l_ref[...]).astype(BF16)
        out = jnp.zeros((t_new, MLA_HEADS * MLA_V), F32)
        for h in range(MLA_HEADS):
            out = out + _dot(lat16[h * t_new:(h + 1) * t_new, :], wuvp_ref[h])
        o_ref[...] = out


def _mla_sample(pt_l, cache_pages, cache_ks_t, q, mlarows, ksnew_t, lw, n_batch, t_new, n_pages):
    pg = PAGES_PER_STEP
    n_groups = n_pages // pg
    rows = MLA_HEADS * t_new
    width = KV_LORA + MLA_ROPE

    def page_spec(k):
        return pl.BlockSpec((None, PAGE_SIZE, width), lambda b, j, pt: (pt[b, j * pg + k], 0, 0))

    def ks_spec(k):
        return pl.BlockSpec((None, MLA_HEADS, PAGE_SIZE), lambda b, j, pt: (pt[b, j * pg + k], 0, 0))

    per_b = lambda w: pl.BlockSpec((t_new, w), lambda b, j, pt: (b, 0))
    const = lambda a: pl.BlockSpec(a.shape, lambda b, j, pt: (0,) * a.ndim)
    grid_spec = pltpu.PrefetchScalarGridSpec(
        num_scalar_prefetch=1,
        grid=(n_batch, n_groups),
        in_specs=[page_spec(k) for k in range(pg)] + [ks_spec(k) for k in range(pg)]
        + [per_b(MLA_HEADS * LANE), per_b(width), pl.BlockSpec((None, MLA_HEADS, t_new), lambda b, j, pt: (b, 0, 0)),
           const(lw["wukt"]), const(lw["wuvp"])],
        out_specs=per_b(MLA_HEADS * MLA_V),
        scratch_shapes=[pltpu.VMEM((rows, KV_LORA), BF16), pltpu.VMEM((rows, width), BF16),
                        pltpu.VMEM((rows, 1), F32), pltpu.VMEM((rows, 1), F32), pltpu.VMEM((rows, KV_LORA), F32)],
    )
    return pl.pallas_call(
        _mla_sample_kernel,
        grid_spec=grid_spec,
        out_shape=jax.ShapeDtypeStruct((n_batch * t_new, MLA_HEADS * MLA_V), F32),
        compiler_params=_params(("parallel", "arbitrary")),
        name="mla_sample",
    )(pt_l, *([cache_pages] * pg), *([cache_ks_t] * pg), q, mlarows, ksnew_t, lw["wukt"], lw["wuvp"])


def _swap_halves(a):
    half = a.shape[-1] // 2
    return jnp.concatenate([a[..., half:], a[..., :half]], axis=-1)


def _layer_weights(p, l):
    w_in = p["w_in"][l]
    zeros = lambda *s: jnp.zeros(s, F32)
    q_cols = jnp.concatenate([w_in[:, :OFF_NSA_KV].reshape(D_MODEL, NSA_HEADS, HEAD_DIM),
                              zeros(D_MODEL, NSA_HEADS, LANE - HEAD_DIM)], axis=-1).reshape(D_MODEL, -1)
    k_rope = w_in[:, OFF_KROPE:IN_COLS]
    e_cols = jnp.concatenate([w_in[:, OFF_GATE:OFF_QLAT], zeros(D_MODEL, HEAD_DIM - 3 * NSA_HEADS),
                              k_rope, _swap_halves(k_rope)], axis=-1)
    w1 = jnp.concatenate([q_cols, w_in[:, OFF_NSA_KV:OFF_GATE], w_in[:, OFF_QLAT:OFF_CKV],
                          w_in[:, OFF_CKV:OFF_KROPE], e_cols], axis=-1).astype(BF16)
    nn = p["nsa_norm"][l]
    ones64 = jnp.ones((HEAD_DIM,), F32)
    gq = jnp.concatenate([nn[0] * NSA_SCALE, zeros(HEAD_DIM)])[None]
    gkv = jnp.concatenate([jnp.stack([jnp.concatenate([nn[2], ones64]), jnp.concatenate([nn[3], ones64])]),
                           zeros(6, LANE)], axis=0)
    qu = p["w_q_up"][l].reshape(Q_LORA, MLA_HEADS, MLA_NOPE + MLA_ROPE)
    wqup = jnp.concatenate([qu, _swap_halves(qu[..., MLA_NOPE:])], axis=-1).reshape(Q_LORA, -1).astype(BF16)
    gn, gr = p["mla_nope_norm"][l], p["mla_rope_norm"][l]
    gmq = jnp.concatenate([gn[0] * gn[1], gr[0], _swap_halves(gr[0])])[None]
    ge = jnp.concatenate([zeros(HEAD_DIM), gr[1], _swap_halves(gr[1])])[None]
    wuk = jnp.concatenate([p["w_uk"][l], zeros(KV_LORA, MLA_HEADS, LANE - MLA_NOPE)], axis=-1)
    wuk = wuk.reshape(KV_LORA, -1).astype(BF16)
    wuv = p["w_uv"][l].reshape(KV_LORA, -1).astype(BF16)
    wukt = jnp.concatenate([jnp.transpose(p["w_uk"][l], (1, 2, 0)), zeros(MLA_HEADS, LANE - MLA_NOPE, KV_LORA)],
                           axis=1).astype(BF16)
    eye = jnp.eye(MLA_HEADS, dtype=F32)
    wuvp = (jnp.transpose(p["w_uv"][l], (1, 0, 2))[:, :, None, :] * eye[:, None, :, None]).reshape(
        MLA_HEADS, KV_LORA, MLA_HEADS * MLA_V).astype(BF16)
    w1r = p["cmp_w1"][l].reshape(2, 2, CMP_STRIDE, HEAD_DIM, CMP_HIDDEN)
    z = zeros(CMP_STRIDE, HEAD_DIM, CMP_HIDDEN)
    k_rows = jnp.concatenate([w1r[0, 0], z, w1r[0, 1], z], axis=-1)
    v_rows = jnp.concatenate([z, w1r[1, 0], z, w1r[1, 1]], axis=-1)
    cw1 = jnp.concatenate([k_rows, v_rows], axis=1).reshape(CMP_STRIDE * LANE, 2 * LANE).astype(BF16)
    pe = p["cmp_pe"][l].reshape(2, 2, CMP_STRIDE, HEAD_DIM)
    pe_rows = jnp.concatenate([pe[0], pe[1]], axis=-1).reshape(2, CMP_STRIDE * LANE)
    cpe = jnp.concatenate([pe_rows, zeros(6, CMP_STRIDE * LANE)], axis=0).astype(BF16)
    w2 = p["cmp_w2"][l]
    z2 = zeros(CMP_HIDDEN, HEAD_DIM)
    cw2 = jnp.concatenate([jnp.concatenate([w2[0], z2], axis=1), jnp.concatenate([z2, w2[1]], axis=1)],
                          axis=0).astype(BF16)
    gkc = jnp.concatenate([nn[1], ones64])[None]
    wup = jnp.transpose(p["w_up"][l].reshape(D_MODEL, 2 * N_FF_CHUNKS, FF_CHUNK), (1, 0, 2)).astype(BF16)
    cwb = jnp.concatenate([p["conv_w"][l], p["conv_b"][l][None], zeros(8 - CONV_W - 1, 2 * D_FF)], axis=0)
    cw = jnp.transpose(cwb.reshape(8, 2 * N_FF_CHUNKS, FF_CHUNK), (1, 0, 2))
    wdn = p["w_down"][l].reshape(N_FF_CHUNKS, FF_CHUNK, D_MODEL).astype(BF16)
    return dict(gattn=p["attn_norm"][l][None], w1=w1, gq=gq, gkv=gkv, gql=p["mla_q_lat_norm"][l][None],
                wqup=wqup, gmq=gmq, gc=p["mla_kv_norm"][l][None], wuk=wuk, wuv=wuv, ge=ge, wukt=wukt, wuvp=wuvp,
                cw1=cw1, cpe=cpe, cw2=cw2, gkc=gkc, wo=p["w_o"][l].astype(BF16), gffn=p["ffn_norm"][l][None],
                wup=wup, cw=cw, wdn=wdn)


def _rope_table(pos):
    half = MLA_ROPE // 2
    inv_freq = ROPE_THETA ** (-jnp.arange(half, dtype=F32) / half)
    ang = pos.astype(F32)[:, None] * inv_freq[None, :]
    cos, sin = jnp.cos(ang), jnp.sin(ang)
    n = pos.shape[0]
    z = lambda w: jnp.zeros((n, w), F32)
    cc = jnp.concatenate([cos, cos], axis=1)
    ss = jnp.concatenate([-sin, sin], axis=1)
    ctab = jnp.concatenate([jnp.full((n, MLA_NOPE), MLA_SCALE, F32), cc * MLA_SCALE, z(MLA_ROPE)], axis=1)
    stab = jnp.concatenate([z(MLA_NOPE + MLA_ROPE), ss * MLA_SCALE], axis=1)
    ck = jnp.concatenate([z(MLA_NOPE), cc, z(MLA_ROPE)], axis=1)
    sk = jnp.concatenate([z(MLA_NOPE + MLA_ROPE), ss], axis=1)
    return jnp.concatenate([ctab, stab, ck, sk], axis=1)


def _cmp_map(n_rows, n_sel, n_lanes):
    c0 = np.arange(n_rows)[:, None] * CMP_STRIDE
    s0 = np.arange(n_lanes)[None, :] * SEL_BLOCK
    m = (c0 < s0 + SEL_BLOCK) & (c0 + CMP_BLOCK > s0) & (np.arange(n_lanes)[None, :] < n_sel)
    return jnp.asarray(m.astype(np.float32), BF16)


def _prompt_tables(rel_bias, seq):
    t = np.arange(Q_BLOCK)[:, None]
    j = np.arange(LANE)[None, :]
    n_var = -(-(_THR[-1] + LANE) // LANE)
    buckets = [_bucket_np(d * LANE + t - j) for d in range(n_var)] + [np.full((Q_BLOCK, LANE), REL_BUCKETS - 1)]
    tbsel = jnp.stack([_bias_rows(rel_bias, b) for b in buckets])
    jw = np.arange(WINDOW + Q_BLOCK)[None, :]
    dist_w = WINDOW + t - jw
    ok = (dist_w >= 0) & (dist_w < WINDOW)
    tbwin = jnp.where(np.tile(ok, (NSA_HEADS, 1)), _bias_rows(rel_bias, _bucket_np(dist_w)), NEG)
    n_cmp = seq // CMP_STRIDE
    cmap = _cmp_map(n_cmp, seq // SEL_BLOCK, LANE)
    return tbsel, tbwin, cmap


def _sample_tables(rel_bias, past, t_new, wb):
    t = np.arange(t_new)[:, None]
    n_cmp = past // CMP_STRIDE
    n = np.arange(n_cmp)[None, :]
    dist_c = past + t - (n * CMP_STRIDE + CMP_BLOCK - 1)
    ok_c = (dist_c >= 0) & (n < n_cmp - 1)
    tbc = jnp.where(np.tile(ok_c, (NSA_HEADS, 1)), _bias_rows(rel_bias, _bucket_np(dist_c)), NEG)
    k = np.arange(past + PAGE_SIZE)[None, :]
    dist_s = past + t - k
    ok_s = (dist_s >= 0) & (k < past + t_new)
    tbs = jnp.where(np.tile(ok_s, (NSA_HEADS, 1)), _bias_rows(rel_bias, _bucket_np(dist_s)), NEG)
    i = np.arange(wb + PAGE_SIZE)[None, :]
    dist_w = wb + t - i
    ok_w = (dist_w >= 0) & (dist_w < WINDOW) & (i < wb + t_new)
    tbw = jnp.where(np.tile(ok_w, (NSA_HEADS, 1)), _bias_rows(rel_bias, _bucket_np(dist_w)), NEG)
    n_sel = past // SEL_BLOCK + 1
    n_lanes = -(-n_sel // LANE) * LANE
    place = np.zeros((NSA_HEADS, LANE, NSA_HEADS * HEAD_DIM), np.float32)
    for h in range(NSA_HEADS):
        place[h, HEAD_DIM + np.arange(HEAD_DIM), h * HEAD_DIM + np.arange(HEAD_DIM)] = 1.0
    return dict(tbc_s=tbc, tbs_s=tbs, tbw_s=tbw, map_s=_cmp_map(n_cmp, n_sel, n_lanes),
                place=jnp.asarray(place, BF16))


def kernel(x_prompt, x_sample, cache_nsa, cache_mla, cache_mla_kscale, state_win, state_conv, page_table,
           rel_bias, attn_norm, w_in, nsa_norm, cmp_pe, cmp_w1, cmp_w2, mla_q_lat_norm, mla_kv_norm,
           w_q_up, w_uk, w_uv, mla_nope_norm, mla_rope_norm, w_o, ffn_norm, w_up, conv_w, conv_b, w_down):
    p = dict(attn_norm=attn_norm, w_in=w_in, nsa_norm=nsa_norm, cmp_pe=cmp_pe, cmp_w1=cmp_w1, cmp_w2=cmp_w2,
             mla_q_lat_norm=mla_q_lat_norm, mla_kv_norm=mla_kv_norm, w_q_up=w_q_up, w_uk=w_uk, w_uv=w_uv,
             mla_nope_norm=mla_nope_norm, mla_rope_norm=mla_rope_norm, w_o=w_o, ffn_norm=ffn_norm, w_up=w_up,
             conv_w=conv_w, conv_b=conv_b, w_down=w_down)
    depth = w_in.shape[0]
    nb, seq, _ = x_prompt.shape
    db, t_new, _ = x_sample.shape
    n_pool = cache_nsa.shape[1]
    n_pages = page_table.shape[1]
    past = n_pages * PAGE_SIZE
    wb = state_win.shape[2]
    assert seq % MLA_TQ == 0 and n_pages % PAGES_PER_STEP == 0 and past % (8 * PAGE_SIZE) == 0
    assert wb == WINDOW and t_new == 8 and seq >= WINDOW

    tm_p = 256
    tm_f = 512 if seq % 512 == 0 else 256
    ts = db * t_new
    tm_s = 128 if ts % 128 == 0 else ts
    rope_p = _rope_table(jnp.arange(seq, dtype=jnp.int32))
    rope_s = jnp.tile(_rope_table(past + jnp.arange(t_new, dtype=jnp.int32)), (db, 1))
    tbsel, tbwin, cmap = _prompt_tables(rel_bias, seq)
    tabs = _sample_tables(rel_bias, past, t_new, wb)
    nsa_pages = cache_nsa.reshape(depth * n_pool, PAGE_SIZE, 4 * HEAD_DIM)
    mla_pages = cache_mla.reshape(depth * n_pool, PAGE_SIZE, KV_LORA + MLA_ROPE)
    ks_pages_t = jnp.swapaxes(cache_mla_kscale.reshape(depth * n_pool, PAGE_SIZE, MLA_HEADS), 1, 2)

    xp = x_prompt.reshape(nb * seq, D_MODEL)
    xs = x_sample.reshape(ts, D_MODEL)
    outs_p = [[] for _ in range(5)]
    outs_s = [[] for _ in range(5)]
    for l in range(depth):
        lw = _layer_weights(p, l)
        (q_nsa, nsarows, winrows, cmp16, sel16, win16, gates, q_mla, mlarows, kscale, kfull, v16) = _proj(
            xp, lw, rope_p, tm_p, BF16)
        kvc = _compress(cmp16.reshape(nb * seq // CMP_STRIDE, CMP_STRIDE * LANE), lw, nb)
        win16p = jnp.pad(win16.reshape(nb, seq, LANE), ((0, 0), (WINDOW, 0), (0, 0)))
        mix_nsa = _nsa_prompt(rel_bias, q_nsa, kvc, sel16, win16p, gates, tbsel, tbwin, cmap, nb, seq)
        mix_mla = _mla_prompt(q_mla, kfull, v16, nb, seq, MLA_TQ)
        xp, tail = _ffn(xp, mix_nsa, mix_mla, lw, tm_f, tiles_per_seq=seq // tm_f)
        tiles = seq // tm_f
        last = tail.reshape(nb, tiles, 2 * N_FF_CHUNKS, 8, FF_CHUNK)[:, tiles - 1, :, 8 - (CONV_W - 1):, :]
        outs_p[0].append(nsarows.reshape(nb, seq, 4, 1, HEAD_DIM))
        outs_p[1].append(mlarows.reshape(nb, seq, KV_LORA + MLA_ROPE))
        outs_p[2].append(kscale[:, :MLA_HEADS].reshape(nb, seq, MLA_HEADS))
        outs_p[3].append(winrows.reshape(nb, seq, 2, 1, HEAD_DIM)[:, seq - min(WINDOW, seq):])
        outs_p[4].append(jnp.transpose(last, (0, 2, 1, 3)).reshape(nb, CONV_W - 1, 2 * D_FF))
        (q_nsa, nsarows, winrows, _, _, _, gates, q_mla, mlarows, kscale, _, _) = _proj(xs, lw, rope_s, ts, F32)
        pt_l = page_table + l * n_pool
        st = state_win[l].reshape(db, wb, LANE)
        mix_nsa = _nsa_sample(pt_l, nsa_pages, q_nsa, nsarows, winrows, st, gates, lw, tabs, db, t_new, n_pages)
        ksnew_t = jnp.swapaxes(kscale[:, :MLA_HEADS].reshape(db, t_new, MLA_HEADS), 1, 2)
        mix_mla = _mla_sample(pt_l, mla_pages, ks_pages_t, q_mla, mlarows, ksnew_t, lw, db, t_new, n_pages)
        sc = state_conv[l]
        zpad = jnp.zeros((db, t_new - 1, 2 * D_FF), F32)
        a1 = jnp.concatenate([sc[:, 1:2], zpad], axis=1)
        a2 = jnp.concatenate([sc, zpad[:, 1:]], axis=1)
        chunked = lambda a: jnp.transpose(a.reshape(ts, 2 * N_FF_CHUNKS, FF_CHUNK), (1, 0, 2))
        xs, tail = _ffn(xs, mix_nsa.astype(BF16), mix_mla.astype(BF16), lw, tm_s, period=t_new,
                        a1=chunked(a1), a2=chunked(a2))
        h_rows = jnp.transpose(tail, (0, 2, 1, 3)).reshape(db, t_new, 2 * D_FF)
        outs_s[0].append(nsarows.reshape(db, t_new, 4, 1, HEAD_DIM))
        outs_s[1].append(mlarows.reshape(db, t_new, KV_LORA + MLA_ROPE))
        outs_s[2].append(kscale[:, :MLA_HEADS].reshape(db, t_new, MLA_HEADS))
        outs_s[3].append(jnp.concatenate([state_win[l][:, t_new:], winrows.reshape(db, t_new, 2, 1, HEAD_DIM)], axis=1))
        outs_s[4].append(h_rows[:, t_new - (CONV_W - 1):])
    stack = lambda o: jnp.stack(o, axis=0)
    nsa_p, mla_p, ks_p, win_p, cv_p = [stack(o) for o in outs_p]
    nsa_s, mla_s, ks_s, win_s, cv_s = [stack(o) for o in outs_s]
    return (xp.reshape(nb, seq, D_MODEL), xs.reshape(db, t_new, D_MODEL), nsa_p, mla_p, ks_p, win_p, cv_p,
            nsa_s, mla_s, ks_s, win_s, cv_s)
```

```python
import functools
import math

import numpy as np
import jax
import jax.numpy as jnp
from jax import lax
from jax.experimental import pallas as pl
from jax.experimental.pallas import tpu as pltpu

F32 = jnp.float32
BF16 = jnp.bfloat16

D_MODEL = 1024
PAGE_SIZE = 128
HEAD_DIM = 64
NSA_HEADS = 8
CMP_STRIDE = 16
CMP_BLOCK = 32
CMP_HIDDEN = 64
SEL_BLOCK = 64
SEL_TOPK = 16
WINDOW = 512
MLA_HEADS = 8
MLA_NOPE = 64
MLA_ROPE = 32
MLA_V = 64
Q_LORA = 384
KV_LORA = 256
ROPE_THETA = 10000.0
D_FF = 2816
CONV_W = 3
REL_BUCKETS = 32
REL_MAX_DIST = 1024
EPS = 1e-6
Q_BLOCK = 128
NSA_SCALE = HEAD_DIM ** -0.5
MLA_SCALE = (MLA_NOPE + MLA_ROPE) ** -0.5

OFF_NSA_KV = NSA_HEADS * HEAD_DIM
OFF_GATE = OFF_NSA_KV + 6 * HEAD_DIM
OFF_QLAT = OFF_GATE + 3 * NSA_HEADS
OFF_CKV = OFF_QLAT + Q_LORA
OFF_KROPE = OFF_CKV + KV_LORA
IN_COLS = OFF_KROPE + MLA_ROPE

LANE = 128
NEG = -1e30
VMEM_LIMIT = 56 * 1024 * 1024

C_Q = 0
C_KV = C_Q + NSA_HEADS * LANE
C_QL = C_KV + 3 * LANE
C_CKV = C_QL + Q_LORA
C_E = C_CKV + KV_LORA
C_END = C_E + LANE

FF_CHUNK = 256
N_FF_CHUNKS = D_FF // FF_CHUNK
PAGES_PER_STEP = 16
SEL_TK = 512
MLA_TQ = 256
MLA_TK = 512


def _dot(a, b):
    return jnp.dot(a, b, preferred_element_type=F32)


def _dot_nt(a, b):
    return lax.dot_general(a, b, (((1,), (1,)), ((), ())), preferred_element_type=F32)


def _params(sem=None):
    return pltpu.CompilerParams(dimension_semantics=sem, vmem_limit_bytes=VMEM_LIMIT)


def _full(shape):
    n = len(shape)
    return pl.BlockSpec(shape, lambda *_: (0,) * n)


def _bucket_thresholds():
    thr = list(range(1, REL_BUCKETS // 2 + 1))
    for k in range(1, REL_BUCKETS // 2):
        d = int(2.0 ** ((3 * k + 32) / 8.0)) - 2
        while d ** 8 < 2 ** (3 * k + 32):
            d += 1
        thr.append(d)
    return thr


_THR = _bucket_thresholds()


def _bucket_np(dist):
    n = np.maximum(np.asarray(dist, np.int64), 0)
    out = np.zeros(n.shape, np.int32)
    for t in _THR:
        out += (n >= t).astype(np.int32)
    return out


def _bias_rows(rel_bias, bucket):
    t, k = bucket.shape
    return jnp.transpose(rel_bias)[:, bucket].reshape(NSA_HEADS * t, k)


def _proj_kernel(x_ref, gattn_ref, w1_ref, gq_ref, gkv_ref, gql_ref, wqup_ref, gmq_ref, gc_ref,
                 wuk_ref, wuv_ref, ge_ref, rope_ref,
                 qnsa_ref, nsarows_ref, winrows_ref, cmp16_ref, sel16_ref, win16_ref, gates_ref,
                 qmla_ref, mlarows_ref, kscale_ref, kfull_ref, v16_ref):
    tm = x_ref.shape[0]
    lane = lax.broadcasted_iota(jnp.int32, (tm, LANE), 1)
    lo = lane < HEAD_DIM
    x = x_ref[...]
    xn = x * lax.rsqrt(jnp.mean(x * x, axis=-1, keepdims=True) + EPS) * gattn_ref[...]
    xn16 = xn.astype(BF16)

    qb = _dot(xn16, w1_ref[:, C_Q:C_KV])
    for h in range(NSA_HEADS):
        b = qb[:, h * LANE:(h + 1) * LANE]
        rs = lax.rsqrt(jnp.sum(b * b, axis=-1, keepdims=True) * (1.0 / HEAD_DIM) + EPS)
        qnsa_ref[:, h * LANE:(h + 1) * LANE] = (b * rs * gq_ref[...]).astype(qnsa_ref.dtype)

    kv = _dot(xn16, w1_ref[:, C_KV:C_QL])
    b0 = kv[:, 0:LANE]
    nsarows_ref[:, 0:LANE] = b0
    cmp16_ref[...] = b0.astype(BF16)
    for j, (dst32, dst16) in enumerate(((nsarows_ref, sel16_ref), (winrows_ref, win16_ref))):
        b = kv[:, (j + 1) * LANE:(j + 2) * LANE]
        ss = jnp.sum(jnp.where(lo, b * b, 0.0), axis=-1, keepdims=True)
        rs = lax.rsqrt(ss * (1.0 / HEAD_DIM) + EPS)
        bn = b * jnp.where(lo, rs, 1.0) * gkv_ref[j:j + 1, :]
        if j == 0:
            dst32[:, LANE:2 * LANE] = bn
        else:
            dst32[...] = bn
        dst16[...] = bn.astype(BF16)

    e = _dot(xn16, w1_ref[:, C_E:C_END])
    gates_ref[...] = 1.0 / (1.0 + jnp.exp(-e))
    rope_lanes = (lane >= HEAD_DIM) & (lane < HEAD_DIM + MLA_ROPE)
    sse = jnp.sum(jnp.where(rope_lanes, e * e, 0.0), axis=-1, keepdims=True)
    en = e * lax.rsqrt(sse * (1.0 / MLA_ROPE) + EPS) * ge_ref[...]
    krblk = en * rope_ref[:, 2 * LANE:3 * LANE] + pltpu.roll(en * rope_ref[:, 3 * LANE:4 * LANE], LANE - MLA_ROPE, 1)
    mlarows_ref[:, KV_LORA:KV_LORA + MLA_ROPE] = pltpu.roll(krblk, HEAD_DIM, 1)[:, 0:MLA_ROPE]

    ql = _dot(xn16, w1_ref[:, C_QL:C_CKV])
    qln = ql * lax.rsqrt(jnp.mean(ql * ql, axis=-1, keepdims=True) + EPS) * gql_ref[...]
    qa = _dot(qln.astype(BF16), wqup_ref[...])
    ctab = rope_ref[:, 0:LANE]
    stab = rope_ref[:, LANE:2 * LANE]
    for h in range(MLA_HEADS):
        b = qa[:, h * LANE:(h + 1) * LANE]
        b2 = b * b
        ssn = jnp.sum(jnp.where(lo, b2, 0.0), axis=-1, keepdims=True)
        ssr = jnp.sum(jnp.where(rope_lanes, b2, 0.0), axis=-1, keepdims=True)
        scale = jnp.where(lo, lax.rsqrt(ssn * (1.0 / MLA_NOPE) + EPS), lax.rsqrt(ssr * (1.0 / MLA_ROPE) + EPS))
        bn = b * scale * gmq_ref[...]
        q = bn * ctab + pltpu.roll(bn * stab, LANE - MLA_ROPE, 1)
        qmla_ref[:, h * LANE:(h + 1) * LANE] = q.astype(qmla_ref.dtype)

    cb = _dot(xn16, w1_ref[:, C_CKV:C_E])
    c = cb * lax.rsqrt(jnp.mean(cb * cb, axis=-1, keepdims=True) + EPS) * gc_ref[...]
    mlarows_ref[:, 0:KV_LORA] = c
    c16 = c.astype(BF16)
    kn = _dot(c16, wuk_ref[...])
    ksacc = jnp.zeros((tm, LANE), F32)
    for h in range(MLA_HEADS):
        b = kn[:, h * LANE:(h + 1) * LANE]
        ksh = lax.rsqrt(jnp.sum(b * b, axis=-1, keepdims=True) * (1.0 / MLA_NOPE) + EPS)
        ksacc = jnp.where(lane == h, ksh, ksacc)
        kfull_ref[:, h * LANE:(h + 1) * LANE] = (b * ksh + krblk).astype(BF16)
    kscale_ref[...] = ksacc
    v16_ref[...] = _dot(c16, wuv_ref[...]).astype(BF16)


def _proj(x, lw, rope_tab, tm, qdtype):
    t = x.shape[0]
    n_rope = rope_tab.shape[0] // tm
    row = lambda w: pl.BlockSpec((tm, w), lambda i: (i, 0))
    sds = lambda w, dt: jax.ShapeDtypeStruct((t, w), dt)
    in_specs = [row(D_MODEL), _full((1, D_MODEL)), _full((D_MODEL, C_END)), _full((1, LANE)), _full((8, LANE)),
                _full((1, Q_LORA)), _full((Q_LORA, MLA_HEADS * LANE)), _full((1, LANE)), _full((1, KV_LORA)),
                _full((KV_LORA, MLA_HEADS * LANE)), _full((KV_LORA, MLA_HEADS * MLA_V)), _full((1, LANE)),
                pl.BlockSpec((tm, 4 * LANE), lambda i: (i % n_rope, 0))]
    widths = [(NSA_HEADS * LANE, qdtype), (2 * LANE, F32), (LANE, F32), (LANE, BF16), (LANE, BF16), (LANE, BF16),
              (LANE, F32), (MLA_HEADS * LANE, qdtype), (KV_LORA + MLA_ROPE, F32), (LANE, F32),
              (MLA_HEADS * LANE, BF16), (MLA_HEADS * MLA_V, BF16)]
    return pl.pallas_call(
        _proj_kernel,
        grid=(t // tm,),
        in_specs=in_specs,
        out_specs=[row(w) for w, _ in widths],
        out_shape=[sds(w, dt) for w, dt in widths],
        compiler_params=_params(("parallel",)),
        name="proj",
    )(x, lw["gattn"], lw["w1"], lw["gq"], lw["gkv"], lw["gql"], lw["wqup"], lw["gmq"], lw["gc"],
      lw["wuk"], lw["wuv"], lw["ge"], rope_tab)


def _compress_tail(h, pe_r, w2, gk, lane_lo):
    n = h.shape[0]
    pe = pe_r[0:1, 0:LANE] + pe_r[1:2, LANE:2 * LANE]
    hid = jax.nn.gelu(h[:, 0:LANE] + pltpu.roll(h[:, LANE:2 * LANE], n - 1, 0) + pe, approximate=True)
    out = _dot(hid.astype(BF16), w2)
    ss = jnp.sum(jnp.where(lane_lo, out * out, 0.0), axis=-1, keepdims=True)
    return out * jnp.where(lane_lo, lax.rsqrt(ss * (1.0 / HEAD_DIM) + EPS), 1.0) * gk


def _compress_kernel(x_ref, w1_ref, pe_ref, w2_ref, gk_ref, out_ref):
    n = x_ref.shape[0]
    lane_lo = lax.broadcasted_iota(jnp.int32, (n, LANE), 1) < HEAD_DIM
    h = _dot(x_ref[...], w1_ref[...])
    pe_r = _dot(pe_ref[...], w1_ref[...])
    out_ref[...] = _compress_tail(h, pe_r, w2_ref[...], gk_ref[...], lane_lo).astype(BF16)


def _compress(chunks, lw, n_batch):
    n = chunks.shape[0] // n_batch
    kw = CMP_STRIDE * LANE
    return pl.pallas_call(
        _compress_kernel,
        grid=(n_batch,),
        in_specs=[pl.BlockSpec((n, kw), lambda b: (b, 0)), _full((kw, 2 * LANE)), _full((8, kw)),
                  _full((LANE, LANE)), _full((1, LANE))],
        out_specs=pl.BlockSpec((n, LANE), lambda b: (b, 0)),
        out_shape=jax.ShapeDtypeStruct((chunks.shape[0], LANE), BF16),
        compiler_params=_params(("parallel",)),
        name="compress",
    )(chunks, lw["cw1"], lw["cpe"], lw["cw2"], lw["gkc"])


def _topk_mask(score, lane):
    sel = jnp.zeros(score.shape, F32)
    cur = score
    for _ in range(SEL_TOPK):
        pick = lane == jnp.argmax(cur, axis=-1, keepdims=True)
        sel = jnp.where(pick, 1.0, sel)
        cur = jnp.where(pick, -jnp.inf, cur)
    return sel


def _softmax_rows(s, valid):
    m = jnp.max(s, axis=-1, keepdims=True)
    e = jnp.where(valid, jnp.exp(s - m), 0.0)
    return e / jnp.maximum(jnp.sum(e, axis=-1, keepdims=True), 1e-30)


def _nsa_prompt_kernel(q_ref, kvc_ref, sel_ref, win_ref, gates_ref, tbcmp_ref, tbsel_ref, tbwin_ref, map_ref,
                       mix_ref, m_ref, acc_ref):
    qi = pl.program_id(1)
    q0 = qi * Q_BLOCK
    n_cmp = kvc_ref.shape[0]
    qblk = q_ref[...]
    qh = [qblk[:, h * LANE:(h + 1) * LANE] for h in range(NSA_HEADS)]

    kvc = kvc_ref[...]
    cw = min(LANE, n_cmp)
    t_i = lax.broadcasted_iota(jnp.int32, (Q_BLOCK, n_cmp), 0)
    n_i = lax.broadcasted_iota(jnp.int32, (Q_BLOCK, n_cmp), 1)
    valid_c = q0 + t_i - (n_i * CMP_STRIDE + (CMP_BLOCK - 1)) >= 0
    ridx = [jnp.clip(qi - (LANE * CMP_STRIDE // Q_BLOCK) * blk, 0, tbcmp_ref.shape[0] - 1)
            for blk in range(n_cmp // cw)]
    cext = jnp.concatenate([kvc, map_ref[...], jnp.ones((n_cmp, LANE), BF16)], axis=1)
    o_c, imp = [], None
    for h in range(NSA_HEADS):
        r = slice(h * Q_BLOCK, (h + 1) * Q_BLOCK)
        tb = jnp.concatenate([tbcmp_ref[i, r, 0:cw] for i in ridx], axis=1)
        s = jnp.where(valid_c, _dot_nt(qh[h], kvc) + tb, NEG)
        e = jnp.where(valid_c, jnp.exp(s - jnp.max(s, axis=-1, keepdims=True)), 0.0)
        res = _dot(e.astype(BF16), cext)
        inv = 1.0 / jnp.maximum(res[:, 2 * LANE:3 * LANE], 1e-30)
        o_c.append(res[:, 0:LANE] * inv)
        imp = res[:, LANE:2 * LANE] * inv if imp is None else imp + res[:, LANE:2 * LANE] * inv

    n_sel = sel_ref.shape[0] // SEL_BLOCK
    lane = lax.broadcasted_iota(jnp.int32, (Q_BLOCK, LANE), 1)
    qpos = q0 + lax.broadcasted_iota(jnp.int32, (Q_BLOCK, LANE), 0)
    q_blk = qpos // SEL_BLOCK
    forced = (lane == 0) | (lane == q_blk) | (lane == q_blk - 1)
    valid_b = lane * SEL_BLOCK <= qpos
    score = jnp.where(valid_b & (lane < n_sel), jnp.where(forced, jnp.inf, imp), -jnp.inf)
    sel16 = _topk_mask(score, lane).astype(BF16)

    n_w = WINDOW + Q_BLOCK
    kvw = win_ref[pl.ds(pl.multiple_of(q0, Q_BLOCK), n_w), :]
    col_w = lax.broadcasted_iota(jnp.int32, (Q_BLOCK, n_w), 1)
    pos_mask = jnp.where(q0 - WINDOW + col_w >= 0, 0.0, NEG)
    wext = jnp.concatenate([kvw, jnp.ones((n_w, LANE), BF16)], axis=1)
    o_w = []
    for h in range(NSA_HEADS):
        s = _dot_nt(qh[h], kvw) + (tbwin_ref[h * Q_BLOCK:(h + 1) * Q_BLOCK, :] + pos_mask)
        e = jnp.exp(s - jnp.max(s, axis=-1, keepdims=True))
        res = _dot(e.astype(BF16), wext)
        o_w.append(res[:, 0:LANE] / res[:, LANE:2 * LANE])

    m_ref[...] = jnp.full(m_ref.shape, NEG, F32)
    acc_ref[...] = jnp.zeros(acc_ref.shape, F32)
    blk_i = lax.broadcasted_iota(jnp.int32, (LANE, SEL_TK), 0)
    col_i = lax.broadcasted_iota(jnp.int32, (LANE, SEL_TK), 1)
    row_t = lax.broadcasted_iota(jnp.int32, (Q_BLOCK, SEL_TK), 0)
    col_t = lax.broadcasted_iota(jnp.int32, (Q_BLOCK, SEL_TK), 1)
    n_const = tbsel_ref.shape[0] - 1
    ones = jnp.ones((SEL_TK, LANE), BF16)

    def sel_step(kt, carry):
        k0 = pl.multiple_of(kt * SEL_TK, SEL_TK)
        kv = sel_ref[pl.ds(k0, SEL_TK), :]
        kvext = jnp.concatenate([kv, ones], axis=1)
        expand = jnp.where(blk_i == (k0 + col_i) // SEL_BLOCK, 1.0, 0.0).astype(BF16)
        mk = _dot(sel16, expand)
        add_mask = jnp.where((mk > 0.5) & (k0 + col_t <= q0 + row_t), 0.0, NEG)
        didx = [jnp.clip((q0 - k0) // LANE - j, 0, n_const) for j in range(SEL_TK // LANE)]
        ss = [_dot_nt(qh[h], kv) for h in range(NSA_HEADS)]
        ps, alphas = [], []
        for h in range(NSA_HEADS):
            r = slice(h * Q_BLOCK, (h + 1) * Q_BLOCK)
            tb = jnp.concatenate([tbsel_ref[d, r, :] for d in didx], axis=1)
            s = ss[h] + (tb + add_mask)
            m_old = m_ref[h]
            m_new = jnp.maximum(m_old, jnp.max(s, axis=-1, keepdims=True))
            alphas.append(jnp.exp(m_old - m_new))
            ps.append(jnp.exp(s - jnp.concatenate([m_new] * (SEL_TK // LANE), axis=1)).astype(BF16))
            m_ref[h] = m_new
        for h in range(NSA_HEADS):
            acc_ref[h] = jnp.concatenate([alphas[h]] * 2, axis=1) * acc_ref[h] + _dot(ps[h], kvext)
        return carry

    lax.fori_loop(0, q0 // SEL_TK + 1, sel_step, 0)

    g = gates_ref[...]
    mixed = []
    for h in range(NSA_HEADS):
        r = slice(h * Q_BLOCK, (h + 1) * Q_BLOCK)
        o_s = acc_ref[h, :, 0:LANE] / acc_ref[h, :, LANE:2 * LANE]
        mixed.append(g[:, h:h + 1] * o_c[h] + g[:, NSA_HEADS + h:NSA_HEADS + h + 1] * o_s
                     + g[:, 2 * NSA_HEADS + h:2 * NSA_HEADS + h + 1] * o_w[h])
    for j in range(NSA_HEADS // 2):
        slab = jnp.where(lane < HEAD_DIM, pltpu.roll(mixed[2 * j], HEAD_DIM, 1), mixed[2 * j + 1])
        mix_ref[:, j * LANE:(j + 1) * LANE] = slab.astype(BF16)


def _nsa_prompt(q, kvc, sel16, win16p, gates, tbcmp, tbsel, tbwin, cmap, n_batch, seq):
    nqb = seq // Q_BLOCK
    n_cmp = seq // CMP_STRIDE
    const = lambda a: pl.BlockSpec(a.shape, lambda b, i: (0,) * a.ndim, pipeline_mode=pl.Buffered(1))
    return pl.pallas_call(
        _nsa_prompt_kernel,
        grid=(n_batch, nqb),
        in_specs=[pl.BlockSpec((Q_BLOCK, NSA_HEADS * LANE), lambda b, i: (b * nqb + i, 0)),
                  pl.BlockSpec((n_cmp, LANE), lambda b, i: (b, 0)),
                  pl.BlockSpec((seq, LANE), lambda b, i: (b, 0)),
                  pl.BlockSpec((None, seq + WINDOW, LANE), lambda b, i: (b, 0, 0)),
                  pl.BlockSpec((Q_BLOCK, LANE), lambda b, i: (b * nqb + i, 0)),
                  const(tbcmp), const(tbsel), const(tbwin), const(cmap)],
        out_specs=pl.BlockSpec((Q_BLOCK, NSA_HEADS * HEAD_DIM), lambda b, i: (b * nqb + i, 0)),
        scratch_shapes=[pltpu.VMEM((NSA_HEADS, Q_BLOCK, LANE), F32), pltpu.VMEM((NSA_HEADS, Q_BLOCK, 2 * LANE), F32)],
        out_shape=jax.ShapeDtypeStruct((n_batch * seq, NSA_HEADS * HEAD_DIM), BF16),
        compiler_params=_params(("parallel", "parallel")),
        name="nsa_prompt",
    )(q, kvc, sel16, win16p, gates, tbcmp, tbsel, tbwin, cmap)


def _mla_prompt_kernel(q_ref, k_ref, v_ref, o_ref, m_ref, acc_ref):
    qi = pl.program_id(1)
    tq = q_ref.shape[0]
    tk = MLA_TK
    q0 = qi * tq
    lane = lax.broadcasted_iota(jnp.int32, (tq, LANE), 1)
    m_ref[...] = jnp.full(m_ref.shape, NEG, F32)
    acc_ref[...] = jnp.zeros(acc_ref.shape, F32)
    ones = jnp.ones((tk, LANE), BF16)

    def step(kt, mask):
        k0 = pl.multiple_of(kt * tk, tk)
        ss = [_dot_nt(q_ref[:, h * LANE:(h + 1) * LANE], k_ref[pl.ds(k0, tk), h * LANE:(h + 1) * LANE])
              for h in range(MLA_HEADS)]
        ps, alphas = [], []
        for h in range(MLA_HEADS):
            s = ss[h] if mask is None else ss[h] + mask
            m_old = m_ref[h]
            m_new = jnp.maximum(m_old, jnp.max(s, axis=-1, keepdims=True))
            alphas.append(jnp.exp(m_old - m_new))
            ps.append(jnp.exp(s - jnp.concatenate([m_new] * (tk // LANE), axis=1)).astype(BF16))
            m_ref[h] = m_new
        for h in range(MLA_HEADS):
            vl = (h // 2) * LANE
            vext = jnp.concatenate([v_ref[pl.ds(k0, tk), vl:vl + LANE], ones], axis=1)
            acc_ref[h] = jnp.concatenate([alphas[h]] * 2, axis=1) * acc_ref[h] + _dot(ps[h], vext)

    def body(kt, carry):
        step(kt, None)
        return carry

    n_full = q0 // tk
    lax.fori_loop(0, n_full, body, 0)
    row_i = lax.broadcasted_iota(jnp.int32, (tq, tk), 0)
    col_i = lax.broadcasted_iota(jnp.int32, (tq, tk), 1)
    step(n_full, jnp.where(n_full * tk + col_i <= q0 + row_i, 0.0, NEG))
    for j in range(MLA_HEADS // 2):
        lo = acc_ref[2 * j, :, 0:LANE] / acc_ref[2 * j, :, LANE:2 * LANE]
        hi = acc_ref[2 * j + 1, :, 0:LANE] / acc_ref[2 * j + 1, :, LANE:2 * LANE]
        o_ref[:, j * LANE:(j + 1) * LANE] = jnp.where(lane < MLA_V, lo, hi).astype(BF16)


def _mla_prompt(q, kfull, v16, n_batch, seq, tq):
    nq = seq // tq
    return pl.pallas_call(
        _mla_prompt_kernel,
        grid=(n_batch, nq),
        in_specs=[pl.BlockSpec((tq, MLA_HEADS * LANE), lambda b, i: (b * nq + i, 0)),
                  pl.BlockSpec((seq, MLA_HEADS * LANE), lambda b, i: (b, 0)),
                  pl.BlockSpec((seq, MLA_HEADS * MLA_V), lambda b, i: (b, 0))],
        out_specs=pl.BlockSpec((tq, MLA_HEADS * MLA_V), lambda b, i: (b * nq + i, 0)),
        out_shape=jax.ShapeDtypeStruct((n_batch * seq, MLA_HEADS * MLA_V), BF16),
        scratch_shapes=[pltpu.VMEM((MLA_HEADS, tq, LANE), F32), pltpu.VMEM((MLA_HEADS, tq, 2 * LANE), F32)],
        compiler_params=_params(("parallel", "parallel")),
        name="mla_prompt",
    )(q, kfull, v16)


def _ffn_kernel(x_ref, mixa_ref, mixb_ref, wo_ref, gffn_ref, wup_ref, cw_ref, wdn_ref, *rest, period, tiles_per_seq):
    if period is None:
        y_ref, tail_ref, hn_ref, h_ref, acc_ref, bufu_ref, bufg_ref, carry_ref = rest
    else:
        a1_ref, a2_ref, y_ref, tail_ref, hn_ref, h_ref, acc_ref, bufu_ref, bufg_ref = rest
    tm = x_ref.shape[0]
    half = wo_ref.shape[0] // 2
    h = x_ref[...] + _dot(mixa_ref[...], wo_ref[0:half, :]) + _dot(mixb_ref[...], wo_ref[half:, :])
    h_ref[...] = h
    hn = h * lax.rsqrt(jnp.mean(h * h, axis=-1, keepdims=True) + EPS) * gffn_ref[...]
    hn_ref[...] = hn.astype(BF16)
    acc_ref[...] = jnp.zeros(acc_ref.shape, F32)
    if period is None:
        first = pl.program_id(0) % tiles_per_seq == 0
    else:
        tmod = lax.broadcasted_iota(jnp.int32, (tm, FF_CHUNK), 0) % period

    def chunk(c, carry):
        hn16 = hn_ref[...]
        convd = []
        for part, buf in ((0, bufu_ref), (1, bufg_ref)):
            idx = part * N_FF_CHUNKS + c
            hc = _dot(hn16, wup_ref[idx])
            buf[8:8 + tm, :] = hc
            tail_ref[idx] = hc[tm - tail_ref.shape[1]:tm, :]
            if period is None:
                prev = carry_ref[idx]
                buf[0:8, :] = jnp.where(first, 0.0, prev)
                carry_ref[idx] = hc[tm - 8:tm, :]
                h1 = buf[7:7 + tm, :]
                h2 = buf[6:6 + tm, :]
            else:
                buf[0:8, :] = jnp.zeros((8, FF_CHUNK), F32)
                h1 = jnp.where(tmod >= 1, buf[7:7 + tm, :], a1_ref[idx])
                h2 = jnp.where(tmod >= 2, buf[6:6 + tm, :], a2_ref[idx])
            cw = cw_ref[idx]
            convd.append(cw[3:4, :] + ((h2 * cw[0:1, :] + h1 * cw[1:2, :]) + hc * cw[2:3, :]))
        u, g = convd
        act = (g * (1.0 / (1.0 + jnp.exp(-g)))) * u
        acc_ref[...] += _dot(act.astype(BF16), wdn_ref[c])
        return carry

    lax.fori_loop(0, N_FF_CHUNKS, chunk, 0)
    y_ref[...] = h_ref[...] + acc_ref[...]


def _ffn(x, mixa, mixb, lw, tm, tiles_per_seq=None, period=None, a1=None, a2=None):
    t = x.shape[0]
    n_tiles = t // tm
    row = lambda w: pl.BlockSpec((tm, w), lambda i: (i, 0))
    once = pl.Buffered(1)
    const = lambda shape: pl.BlockSpec(shape, lambda i: (0,) * len(shape), pipeline_mode=once)
    in_specs = [row(D_MODEL), row(NSA_HEADS * HEAD_DIM), row(MLA_HEADS * MLA_V),
                const((D_MODEL, D_MODEL)), const((1, D_MODEL)),
                const((2 * N_FF_CHUNKS, D_MODEL, FF_CHUNK)), const((2 * N_FF_CHUNKS, 8, FF_CHUNK)),
                const((N_FF_CHUNKS, FF_CHUNK, D_MODEL))]
    args = [x, mixa, mixb, lw["wo"], lw["gffn"], lw["wup"], lw["cw"], lw["wdn"]]
    tail_rows = 8 if period is None else tm
    scratch = [pltpu.VMEM((tm, D_MODEL), BF16), pltpu.VMEM((tm, D_MODEL), F32), pltpu.VMEM((tm, D_MODEL), F32),
               pltpu.VMEM((tm + 8, FF_CHUNK), F32), pltpu.VMEM((tm + 8, FF_CHUNK), F32)]
    if period is None:
        scratch.append(pltpu.VMEM((2 * N_FF_CHUNKS, 8, FF_CHUNK), F32))
    else:
        in_specs += [pl.BlockSpec((2 * N_FF_CHUNKS, tm, FF_CHUNK), lambda i: (0, i, 0))] * 2
        args += [a1, a2]
    return pl.pallas_call(
        functools.partial(_ffn_kernel, period=period, tiles_per_seq=tiles_per_seq),
        grid=(n_tiles,),
        in_specs=in_specs,
        out_specs=[row(D_MODEL), pl.BlockSpec((None, 2 * N_FF_CHUNKS, tail_rows, FF_CHUNK), lambda i: (i, 0, 0, 0))],
        out_shape=[jax.ShapeDtypeStruct((t, D_MODEL), F32),
                   jax.ShapeDtypeStruct((n_tiles, 2 * N_FF_CHUNKS, tail_rows, FF_CHUNK), F32)],
        scratch_shapes=scratch,
        compiler_params=_params(("arbitrary",)),
        name="ffn",
    )(*args)


def _nsa_sample_kernel(pt_ref, *refs):
    del pt_ref
    pg = PAGES_PER_STEP
    pages = refs[:pg]
    (q_ref, rows_ref, winnew_ref, state_ref, gates_ref, w1_ref, pe_ref, w2_ref, gk_ref, map_ref,
     tbc_ref, tbs_ref, tbw_ref, place_ref, mix_ref, hs_ref, sel_ref, stage_ref, m_ref, l_ref, acc_ref) = refs[pg:]
    j = pl.program_id(1)
    n_groups = pl.num_programs(1)
    n_pages = sel_ref.shape[0] - 1
    past = n_pages * PAGE_SIZE
    t_new = q_ref.shape[0]
    chunks_per_page = PAGE_SIZE // CMP_STRIDE

    xs = []
    for k in range(pg):
        stage_ref[k] = jnp.transpose(pages[k][0:LANE, :])
        xs.append(jnp.concatenate([stage_ref[k, pl.ds(r, chunks_per_page, stride=CMP_STRIDE), :]
                                   for r in range(CMP_STRIDE)], axis=1))
        sel_ref[j * pg + k] = pages[k][LANE:2 * LANE, :].astype(BF16)
    x = jnp.concatenate(xs, axis=0).astype(BF16)
    hbase = pl.multiple_of(j * (pg * chunks_per_page), pg * chunks_per_page)
    hs_ref[pl.ds(hbase, pg * chunks_per_page), :] = _dot(x, w1_ref[...])

    @pl.when(j == n_groups - 1)
    def _():
        n_cmp = hs_ref.shape[0]
        lane_c = lax.broadcasted_iota(jnp.int32, (n_cmp, LANE), 1) < HEAD_DIM
        pe_r = _dot(pe_ref[...], w1_ref[...])
        kvc = _compress_tail(hs_ref[...], pe_r, w2_ref[...], gk_ref[...], lane_c).astype(BF16)
        qblk = q_ref[...]
        qs = jnp.concatenate([qblk[:, h * LANE:(h + 1) * LANE] for h in range(NSA_HEADS)], axis=0).astype(BF16)

        tbc = tbc_ref[...]
        valid_c = tbc > 0.5 * NEG
        p16 = _softmax_rows(jnp.where(valid_c, _dot_nt(qs, kvc) + tbc, NEG), valid_c).astype(BF16)
        o_c = _dot(p16, kvc)
        imp_h = _dot(p16, map_ref[...])
        imp = imp_h[0:t_new, :]
        for h in range(1, NSA_HEADS):
            imp = imp + imp_h[h * t_new:(h + 1) * t_new, :]

        n_lane = map_ref.shape[1]
        n_sel = past // SEL_BLOCK + 1
        lane = lax.broadcasted_iota(jnp.int32, (t_new, n_lane), 1)
        qpos = past + lax.broadcasted_iota(jnp.int32, (t_new, n_lane), 0)
        q_blk = qpos // SEL_BLOCK
        forced = (lane == 0) | (lane == q_blk) | (lane == q_blk - 1)
        valid_b = lane * SEL_BLOCK <= qpos
        score = jnp.where(valid_b & (lane < n_sel), jnp.where(forced, jnp.inf, imp), -jnp.inf)
        sel = _topk_mask(score, lane)
        sel16 = jnp.concatenate([sel] * NSA_HEADS, axis=0).astype(BF16)

        pad = jnp.zeros((PAGE_SIZE - t_new, LANE), F32)
        sel_ref[n_pages] = jnp.transpose(jnp.concatenate([rows_ref[:, LANE:2 * LANE], pad], axis=0)).astype(BF16)
        m_ref[...] = jnp.full(m_ref.shape, NEG, F32)
        l_ref[...] = jnp.zeros(l_ref.shape, F32)
        acc_ref[...] = jnp.zeros(acc_ref.shape, F32)
        cp = 8

        def sel_chunk(p0, n_pg):
            width = n_pg * PAGE_SIZE
            k0 = p0 * PAGE_SIZE
            kvt = jnp.concatenate([sel_ref[p0 + i] for i in range(n_pg)], axis=1)
            blk_i = lax.broadcasted_iota(jnp.int32, (n_lane, width), 0)
            col_i = lax.broadcasted_iota(jnp.int32, (n_lane, width), 1)
            expand = jnp.where(blk_i == (k0 + col_i) // SEL_BLOCK, 1.0, 0.0).astype(BF16)
            add_mask = jnp.where(_dot(sel16, expand) > 0.5, 0.0, NEG)
            s = _dot(qs, kvt) + (tbs_ref[:, pl.ds(pl.multiple_of(k0, PAGE_SIZE), width)] + add_mask)
            m_old = m_ref[...]
            m_new = jnp.maximum(m_old, jnp.max(s, axis=-1, keepdims=True))
            alpha = jnp.exp(m_old - m_new)
            p = jnp.exp(s - m_new)
            l_ref[...] = alpha * l_ref[...] + jnp.sum(p, axis=-1, keepdims=True)
            acc_ref[...] = alpha * acc_ref[...] + _dot_nt(p.astype(BF16), kvt)
            m_ref[...] = m_new

        def sel_step(c, carry):
            sel_chunk(c * cp, cp)
            return carry

        lax.fori_loop(0, n_pages // cp, sel_step, 0)
        sel_chunk(n_pages, 1)
        o_s = acc_ref[...] / l_ref[...]

        newt = jnp.transpose(jnp.concatenate([winnew_ref[...], pad], axis=0))
        kvwt = jnp.concatenate([state_ref[...], newt], axis=1).astype(BF16)
        sw = _dot(qs, kvwt) + tbw_ref[...]
        ew = jnp.exp(sw - jnp.max(sw, axis=-1, keepdims=True))
        pw = ew / jnp.sum(ew, axis=-1, keepdims=True)
        o_w = _dot_nt(pw.astype(BF16), kvwt)

        g = gates_ref[...]
        out = jnp.zeros((t_new, NSA_HEADS * HEAD_DIM), F32)
        for h in range(NSA_HEADS):
            r = slice(h * t_new, (h + 1) * t_new)
            mh = (g[:, h:h + 1] * o_c[r] + g[:, NSA_HEADS + h:NSA_HEADS + h + 1] * o_s[r]
                  + g[:, 2 * NSA_HEADS + h:2 * NSA_HEADS + h + 1] * o_w[r])
            out = out + _dot(mh.astype(BF16), place_ref[h])
        mix_ref[...] = out


def _nsa_sample(pt_l, pages_t, q, nsarows, winrows, state_t, gates, lw, tabs, n_batch, t_new, n_pages):
    pg = PAGES_PER_STEP
    n_groups = n_pages // pg
    rows = NSA_HEADS * t_new
    n_cmp = n_pages * PAGE_SIZE // CMP_STRIDE

    def page_spec(k):
        return pl.BlockSpec((None, 4 * HEAD_DIM, PAGE_SIZE), lambda b, j, pt: (pt[b, j * pg + k], 0, 0))

    per_b = lambda w: pl.BlockSpec((t_new, w), lambda b, j, pt: (b, 0))
    const = lambda a: pl.BlockSpec(a.shape, lambda b, j, pt: (0,) * a.ndim)
    consts = [lw["cw1"], lw["cpe"], lw["cw2"], lw["gkc"], tabs["map_s"], tabs["tbc_s"], tabs["tbs_s"], tabs["tbw_s"],
              tabs["place"]]
    grid_spec = pltpu.PrefetchScalarGridSpec(
        num_scalar_prefetch=1,
        grid=(n_batch, n_groups),
        in_specs=[page_spec(k) for k in range(pg)]
        + [per_b(NSA_HEADS * LANE), per_b(2 * LANE), per_b(LANE),
           pl.BlockSpec((None, LANE, state_t.shape[2]), lambda b, j, pt: (b, 0, 0)), per_b(LANE)]
        + [const(a) for a in consts],
        out_specs=per_b(NSA_HEADS * HEAD_DIM),
        scratch_shapes=[pltpu.VMEM((n_cmp, 2 * LANE), F32), pltpu.VMEM((n_pages + 1, LANE, PAGE_SIZE), BF16),
                        pltpu.VMEM((pg, PAGE_SIZE, LANE), F32),
                        pltpu.VMEM((rows, 1), F32), pltpu.VMEM((rows, 1), F32), pltpu.VMEM((rows, LANE), F32)],
    )
    return pl.pallas_call(
        _nsa_sample_kernel,
        grid_spec=grid_spec,
        out_shape=jax.ShapeDtypeStruct((n_batch * t_new, NSA_HEADS * HEAD_DIM), F32),
        compiler_params=_params(("parallel", "arbitrary")),
        name="nsa_sample",
    )(pt_l, *([pages_t] * pg), q, nsarows, winrows, state_t, gates, *consts)


def _mla_sample_kernel(pt_ref, *refs):
    del pt_ref
    pg = PAGES_PER_STEP
    pages = refs[:pg]
    kss = refs[pg:2 * pg]
    (q_ref, new_ref, ksnew_ref, wukt_ref, wuvp_ref, o_ref, qa_ref, qr_ref, m_ref, l_ref, acc_ref) = refs[2 * pg:]
    j = pl.program_id(1)
    n_groups = pl.num_programs(1)
    t_new = q_ref.shape[0]
    rows = MLA_HEADS * t_new

    @pl.when(j == 0)
    def _():
        qblk = q_ref[...]
        lane = lax.broadcasted_iota(jnp.int32, (t_new, LANE), 1)
        rope_lanes = (lane >= MLA_NOPE) & (lane < MLA_NOPE + MLA_ROPE)
        for h in range(MLA_HEADS):
            qh = qblk[:, h * LANE:(h + 1) * LANE]
            qa_ref[h * t_new:(h + 1) * t_new, :] = _dot(qh.astype(BF16), wukt_ref[h]).astype(BF16)
            qr = pltpu.roll(jnp.where(rope_lanes, qh, 0.0), LANE - MLA_NOPE, 1)
            qr_ref[h * t_new:(h + 1) * t_new, :] = qr[:, 0:MLA_ROPE].astype(BF16)
        m_ref[...] = jnp.full(m_ref.shape, NEG, F32)
        l_ref[...] = jnp.zeros(l_ref.shape, F32)
        acc_ref[...] = jnp.zeros(acc_ref.shape, F32)

    def update(s, pv):
        m_old = m_ref[...]
        m_new = jnp.maximum(m_old, jnp.max(s, axis=-1, keepdims=True))
        alpha = jnp.exp(m_old - m_new)
        p = jnp.exp(s - m_new)
        l_ref[...] = alpha * l_ref[...] + jnp.sum(p, axis=-1, keepdims=True)
        acc_ref[...] = alpha * acc_ref[...] + pv(p.astype(BF16))
        m_ref[...] = m_new

    def head_rows(ks):
        return jnp.concatenate([jnp.broadcast_to(ks[h:h + 1, :], (t_new, ks.shape[1])) for h in range(MLA_HEADS)],
                               axis=0)

    ct = jnp.concatenate([pages[k][...] for k in range(pg)], axis=1).astype(BF16)
    ks = jnp.concatenate([kss[k][...] for k in range(pg)], axis=1)
    ct_lat = ct[0:KV_LORA, :]
    update(_dot(qa_ref[...], ct_lat) * head_rows(ks) + _dot(qr_ref[...], ct[KV_LORA:, :]),
           lambda p16: _dot_nt(p16, ct_lat))

    @pl.when(j == n_groups - 1)
    def _():
        new16 = new_ref[...].astype(BF16)
        new_lat = new16[:, 0:KV_LORA]
        r_t = lax.broadcasted_iota(jnp.int32, (rows, t_new), 0) % t_new
        c_t = lax.broadcasted_iota(jnp.int32, (rows, t_new), 1)
        s_new = (_dot_nt(qa_ref[...], new_lat) * head_rows(ksnew_ref[...])
                 + _dot_nt(qr_ref[...], new16[:, KV_LORA:]))
        update(s_new + jnp.where(c_t <= r_t, 0.0, NEG), lambda p16: _dot(p16, new_lat))
        lat16 = (acc_ref[...] / l_ref[...]).astype(BF16)
        out = jnp.zeros((t_new, MLA_HEADS * MLA_V), F32)
        for h in range(MLA_HEADS):
            out = out + _dot(lat16[h * t_new:(h + 1) * t_new, :], wuvp_ref[h])
        o_ref[...] = out


def _mla_sample(pt_l, cache_pages, cache_ks_t, q, mlarows, ksnew_t, lw, n_batch, t_new, n_pages):
    pg = PAGES_PER_STEP
    n_groups = n_pages // pg
    rows = MLA_HEADS * t_new
    width = KV_LORA + MLA_ROPE

    def page_spec(k):
        return pl.BlockSpec((None, width, PAGE_SIZE), lambda b, j, pt: (pt[b, j * pg + k], 0, 0))

    def ks_spec(k):
        return pl.BlockSpec((None, MLA_HEADS, PAGE_SIZE), lambda b, j, pt: (pt[b, j * pg + k], 0, 0))

    per_b = lambda w: pl.BlockSpec((t_new, w), lambda b, j, pt: (b, 0))
    const = lambda a: pl.BlockSpec(a.shape, lambda b, j, pt: (0,) * a.ndim)
    grid_spec = pltpu.PrefetchScalarGridSpec(
        num_scalar_prefetch=1,
        grid=(n_batch, n_groups),
        in_specs=[page_spec(k) for k in range(pg)] + [ks_spec(k) for k in range(pg)]
        + [per_b(MLA_HEADS * LANE), per_b(width), pl.BlockSpec((None, MLA_HEADS, t_new), lambda b, j, pt: (b, 0, 0)),
           const(lw["wukt"]), const(lw["wuvp"])],
        out_specs=per_b(MLA_HEADS * MLA_V),
        scratch_shapes=[pltpu.VMEM((rows, KV_LORA), BF16), pltpu.VMEM((rows, MLA_ROPE), BF16),
                        pltpu.VMEM((rows, 1), F32), pltpu.VMEM((rows, 1), F32), pltpu.VMEM((rows, KV_LORA), F32)],
    )
    return pl.pallas_call(
        _mla_sample_kernel,
        grid_spec=grid_spec,
        out_shape=jax.ShapeDtypeStruct((n_batch * t_new, MLA_HEADS * MLA_V), F32),
        compiler_params=_params(("parallel", "arbitrary")),
        name="mla_sample",
    )(pt_l, *([cache_pages] * pg), *([cache_ks_t] * pg), q, mlarows, ksnew_t, lw["wukt"], lw["wuvp"])


def _swap_halves(a):
    half = a.shape[-1] // 2
    return jnp.concatenate([a[..., half:], a[..., :half]], axis=-1)


def _layer_weights(p, l):
    w_in = p["w_in"][l]
    zeros = lambda *s: jnp.zeros(s, F32)
    q_cols = jnp.concatenate([w_in[:, :OFF_NSA_KV].reshape(D_MODEL, NSA_HEADS, HEAD_DIM),
                              zeros(D_MODEL, NSA_HEADS, LANE - HEAD_DIM)], axis=-1).reshape(D_MODEL, -1)
    k_rope = w_in[:, OFF_KROPE:IN_COLS]
    e_cols = jnp.concatenate([w_in[:, OFF_GATE:OFF_QLAT], zeros(D_MODEL, HEAD_DIM - 3 * NSA_HEADS),
                              k_rope, _swap_halves(k_rope)], axis=-1)
    w1 = jnp.concatenate([q_cols, w_in[:, OFF_NSA_KV:OFF_GATE], w_in[:, OFF_QLAT:OFF_CKV],
                          w_in[:, OFF_CKV:OFF_KROPE], e_cols], axis=-1).astype(BF16)
    nn = p["nsa_norm"][l]
    ones64 = jnp.ones((HEAD_DIM,), F32)
    gq = jnp.concatenate([nn[0] * NSA_SCALE, zeros(HEAD_DIM)])[None]
    gkv = jnp.concatenate([jnp.stack([jnp.concatenate([nn[2], ones64]), jnp.concatenate([nn[3], ones64])]),
                           zeros(6, LANE)], axis=0)
    qu = p["w_q_up"][l].reshape(Q_LORA, MLA_HEADS, MLA_NOPE + MLA_ROPE)
    wqup = jnp.concatenate([qu, _swap_halves(qu[..., MLA_NOPE:])], axis=-1).reshape(Q_LORA, -1).astype(BF16)
    gn, gr = p["mla_nope_norm"][l], p["mla_rope_norm"][l]
    gmq = jnp.concatenate([gn[0] * gn[1], gr[0], _swap_halves(gr[0])])[None]
    ge = jnp.concatenate([zeros(HEAD_DIM), gr[1], _swap_halves(gr[1])])[None]
    wuk = jnp.concatenate([p["w_uk"][l], zeros(KV_LORA, MLA_HEADS, LANE - MLA_NOPE)], axis=-1)
    wuk = wuk.reshape(KV_LORA, -1).astype(BF16)
    wuv = p["w_uv"][l].reshape(KV_LORA, -1).astype(BF16)
    wukt = jnp.concatenate([jnp.transpose(p["w_uk"][l], (1, 2, 0)), zeros(MLA_HEADS, LANE - MLA_NOPE, KV_LORA)],
                           axis=1).astype(BF16)
    eye = jnp.eye(MLA_HEADS, dtype=F32)
    wuvp = (jnp.transpose(p["w_uv"][l], (1, 0, 2))[:, :, None, :] * eye[:, None, :, None]).reshape(
        MLA_HEADS, KV_LORA, MLA_HEADS * MLA_V).astype(BF16)
    w1r = p["cmp_w1"][l].reshape(2, 2, CMP_STRIDE, HEAD_DIM, CMP_HIDDEN)
    z = zeros(CMP_STRIDE, HEAD_DIM, CMP_HIDDEN)
    k_rows = jnp.concatenate([w1r[0, 0], z, w1r[0, 1], z], axis=-1)
    v_rows = jnp.concatenate([z, w1r[1, 0], z, w1r[1, 1]], axis=-1)
    cw1 = jnp.concatenate([k_rows, v_rows], axis=1).reshape(CMP_STRIDE * LANE, 2 * LANE).astype(BF16)
    pe = p["cmp_pe"][l].reshape(2, 2, CMP_STRIDE, HEAD_DIM)
    pe_rows = jnp.concatenate([pe[0], pe[1]], axis=-1).reshape(2, CMP_STRIDE * LANE)
    cpe = jnp.concatenate([pe_rows, zeros(6, CMP_STRIDE * LANE)], axis=0).astype(BF16)
    w2 = p["cmp_w2"][l]
    z2 = zeros(CMP_HIDDEN, HEAD_DIM)
    cw2 = jnp.concatenate([jnp.concatenate([w2[0], z2], axis=1), jnp.concatenate([z2, w2[1]], axis=1)],
                          axis=0).astype(BF16)
    gkc = jnp.concatenate([nn[1], ones64])[None]
    wup = jnp.transpose(p["w_up"][l].reshape(D_MODEL, 2 * N_FF_CHUNKS, FF_CHUNK), (1, 0, 2)).astype(BF16)
    cwb = jnp.concatenate([p["conv_w"][l], p["conv_b"][l][None], zeros(8 - CONV_W - 1, 2 * D_FF)], axis=0)
    cw = jnp.transpose(cwb.reshape(8, 2 * N_FF_CHUNKS, FF_CHUNK), (1, 0, 2))
    wdn = p["w_down"][l].reshape(N_FF_CHUNKS, FF_CHUNK, D_MODEL).astype(BF16)
    return dict(gattn=p["attn_norm"][l][None], w1=w1, gq=gq, gkv=gkv, gql=p["mla_q_lat_norm"][l][None],
                wqup=wqup, gmq=gmq, gc=p["mla_kv_norm"][l][None], wuk=wuk, wuv=wuv, ge=ge, wukt=wukt, wuvp=wuvp,
                cw1=cw1, cpe=cpe, cw2=cw2, gkc=gkc, wo=p["w_o"][l].astype(BF16), gffn=p["ffn_norm"][l][None],
                wup=wup, cw=cw, wdn=wdn)


def _rope_table(pos):
    half = MLA_ROPE // 2
    inv_freq = ROPE_THETA ** (-jnp.arange(half, dtype=F32) / half)
    ang = pos.astype(F32)[:, None] * inv_freq[None, :]
    cos, sin = jnp.cos(ang), jnp.sin(ang)
    n = pos.shape[0]
    z = lambda w: jnp.zeros((n, w), F32)
    cc = jnp.concatenate([cos, cos], axis=1)
    ss = jnp.concatenate([-sin, sin], axis=1)
    ctab = jnp.concatenate([jnp.full((n, MLA_NOPE), MLA_SCALE, F32), cc * MLA_SCALE, z(MLA_ROPE)], axis=1)
    stab = jnp.concatenate([z(MLA_NOPE + MLA_ROPE), ss * MLA_SCALE], axis=1)
    ck = jnp.concatenate([z(MLA_NOPE), cc, z(MLA_ROPE)], axis=1)
    sk = jnp.concatenate([z(MLA_NOPE + MLA_ROPE), ss], axis=1)
    return jnp.concatenate([ctab, stab, ck, sk], axis=1)


def _cmp_map(n_rows, n_sel, n_lanes):
    c0 = np.arange(n_rows)[:, None] * CMP_STRIDE
    s0 = np.arange(n_lanes)[None, :] * SEL_BLOCK
    m = (c0 < s0 + SEL_BLOCK) & (c0 + CMP_BLOCK > s0) & (np.arange(n_lanes)[None, :] < n_sel)
    return jnp.asarray(m.astype(np.float32), BF16)


def _prompt_tables(rel_bias, seq):
    t = np.arange(Q_BLOCK)[:, None]
    j = np.arange(LANE)[None, :]
    n_var = -(-(_THR[-1] + LANE) // LANE)
    buckets = [_bucket_np(d * LANE + t - j) for d in range(n_var)] + [np.full((Q_BLOCK, LANE), REL_BUCKETS - 1)]
    tbsel = jnp.stack([_bias_rows(rel_bias, b) for b in buckets])
    jw = np.arange(WINDOW + Q_BLOCK)[None, :]
    dist_w = WINDOW + t - jw
    ok = (dist_w >= 0) & (dist_w < WINDOW)
    tbwin = jnp.where(np.tile(ok, (NSA_HEADS, 1)), _bias_rows(rel_bias, _bucket_np(dist_w)), NEG)
    n_cmp = seq // CMP_STRIDE
    cmap = _cmp_map(n_cmp, seq // SEL_BLOCK, LANE)
    far = _THR[-1] + CMP_STRIDE * (LANE - 1) + CMP_BLOCK - 1
    n_varc = -(-far // Q_BLOCK)
    buckets = [_bucket_np(k * Q_BLOCK + t - (j * CMP_STRIDE + CMP_BLOCK - 1)) for k in range(n_varc)]
    buckets.append(np.full((Q_BLOCK, LANE), REL_BUCKETS - 1))
    tbcmp = jnp.stack([_bias_rows(rel_bias, b) for b in buckets])
    return tbcmp, tbsel, tbwin, cmap


def _sample_tables(rel_bias, past, t_new, wb):
    t = np.arange(t_new)[:, None]
    n_cmp = past // CMP_STRIDE
    n = np.arange(n_cmp)[None, :]
    dist_c = past + t - (n * CMP_STRIDE + CMP_BLOCK - 1)
    ok_c = (dist_c >= 0) & (n < n_cmp - 1)
    tbc = jnp.where(np.tile(ok_c, (NSA_HEADS, 1)), _bias_rows(rel_bias, _bucket_np(dist_c)), NEG)
    k = np.arange(past + PAGE_SIZE)[None, :]
    dist_s = past + t - k
    ok_s = (dist_s >= 0) & (k < past + t_new)
    tbs = jnp.where(np.tile(ok_s, (NSA_HEADS, 1)), _bias_rows(rel_bias, _bucket_np(dist_s)), NEG)
    i = np.arange(wb + PAGE_SIZE)[None, :]
    dist_w = wb + t - i
    ok_w = (dist_w >= 0) & (dist_w < WINDOW) & (i < wb + t_new)
    tbw = jnp.where(np.tile(ok_w, (NSA_HEADS, 1)), _bias_rows(rel_bias, _bucket_np(dist_w)), NEG)
    n_sel = past // SEL_BLOCK + 1
    n_lanes = -(-n_sel // LANE) * LANE
    place = np.zeros((NSA_HEADS, LANE, NSA_HEADS * HEAD_DIM), np.float32)
    for h in range(NSA_HEADS):
        place[h, HEAD_DIM + np.arange(HEAD_DIM), h * HEAD_DIM + np.arange(HEAD_DIM)] = 1.0
    return dict(tbc_s=tbc, tbs_s=tbs, tbw_s=tbw, map_s=_cmp_map(n_cmp, n_sel, n_lanes),
                place=jnp.asarray(place, BF16))


def kernel(x_prompt, x_sample, cache_nsa, cache_mla, cache_mla_kscale, state_win, state_conv, page_table,
           rel_bias, attn_norm, w_in, nsa_norm, cmp_pe, cmp_w1, cmp_w2, mla_q_lat_norm, mla_kv_norm,
           w_q_up, w_uk, w_uv, mla_nope_norm, mla_rope_norm, w_o, ffn_norm, w_up, conv_w, conv_b, w_down):
    p = dict(attn_norm=attn_norm, w_in=w_in, nsa_norm=nsa_norm, cmp_pe=cmp_pe, cmp_w1=cmp_w1, cmp_w2=cmp_w2,
             mla_q_lat_norm=mla_q_lat_norm, mla_kv_norm=mla_kv_norm, w_q_up=w_q_up, w_uk=w_uk, w_uv=w_uv,
             mla_nope_norm=mla_nope_norm, mla_rope_norm=mla_rope_norm, w_o=w_o, ffn_norm=ffn_norm, w_up=w_up,
             conv_w=conv_w, conv_b=conv_b, w_down=w_down)
    depth = w_in.shape[0]
    nb, seq, _ = x_prompt.shape
    db, t_new, _ = x_sample.shape
    n_pool = cache_nsa.shape[1]
    n_pages = page_table.shape[1]
    past = n_pages * PAGE_SIZE
    wb = state_win.shape[2]
    assert seq % MLA_TQ == 0 and n_pages % PAGES_PER_STEP == 0 and past % (8 * PAGE_SIZE) == 0
    assert wb == WINDOW and t_new == 8 and seq >= WINDOW

    tm_p = 256
    tm_f = 512 if seq % 512 == 0 else 256
    ts = db * t_new
    tm_s = 128 if ts % 128 == 0 else ts
    rope_p = _rope_table(jnp.arange(seq, dtype=jnp.int32))
    rope_s = jnp.tile(_rope_table(past + jnp.arange(t_new, dtype=jnp.int32)), (db, 1))
    tbcmp, tbsel, tbwin, cmap = _prompt_tables(rel_bias, seq)
    tabs = _sample_tables(rel_bias, past, t_new, wb)
    nsa_pages_t = jnp.transpose(cache_nsa, (0, 1, 3, 4, 5, 2)).reshape(depth * n_pool, 4 * HEAD_DIM, PAGE_SIZE)
    mla_pages_t = jnp.transpose(cache_mla, (0, 1, 3, 2)).reshape(depth * n_pool, KV_LORA + MLA_ROPE, PAGE_SIZE)
    ks_pages_t = jnp.transpose(cache_mla_kscale, (0, 1, 3, 2)).reshape(depth * n_pool, MLA_HEADS, PAGE_SIZE)
    state_t = jnp.transpose(state_win, (0, 1, 3, 4, 5, 2)).reshape(depth, db, LANE, wb)

    xp = x_prompt.reshape(nb * seq, D_MODEL)
    xs = x_sample.reshape(ts, D_MODEL)
    outs_p = [[] for _ in range(5)]
    outs_s = [[] for _ in range(5)]
    for l in range(depth):
        lw = _layer_weights(p, l)
        (q_nsa, nsarows, winrows, cmp16, sel16, win16, gates, q_mla, mlarows, kscale, kfull, v16) = _proj(
            xp, lw, rope_p, tm_p, BF16)
        kvc = _compress(cmp16.reshape(nb * seq // CMP_STRIDE, CMP_STRIDE * LANE), lw, nb)
        win16p = jnp.pad(win16.reshape(nb, seq, LANE), ((0, 0), (WINDOW, 0), (0, 0)))
        mix_nsa = _nsa_prompt(q_nsa, kvc, sel16, win16p, gates, tbcmp, tbsel, tbwin, cmap, nb, seq)
        mix_mla = _mla_prompt(q_mla, kfull, v16, nb, seq, MLA_TQ)
        xp, tail = _ffn(xp, mix_nsa, mix_mla, lw, tm_f, tiles_per_seq=seq // tm_f)
        tiles = seq // tm_f
        last = tail.reshape(nb, tiles, 2 * N_FF_CHUNKS, 8, FF_CHUNK)[:, tiles - 1, :, 8 - (CONV_W - 1):, :]
        outs_p[0].append(nsarows.reshape(nb, seq, 4, 1, HEAD_DIM))
        outs_p[1].append(mlarows.reshape(nb, seq, KV_LORA + MLA_ROPE))
        outs_p[2].append(kscale[:, :MLA_HEADS].reshape(nb, seq, MLA_HEADS))
        outs_p[3].append(winrows.reshape(nb, seq, 2, 1, HEAD_DIM)[:, seq - min(WINDOW, seq):])
        outs_p[4].append(jnp.transpose(last, (0, 2, 1, 3)).reshape(nb, CONV_W - 1, 2 * D_FF))
        (q_nsa, nsarows, winrows, _, _, _, gates, q_mla, mlarows, kscale, _, _) = _proj(xs, lw, rope_s, ts, F32)
        pt_l = page_table + l * n_pool
        mix_nsa = _nsa_sample(pt_l, nsa_pages_t, q_nsa, nsarows, winrows, state_t[l], gates, lw, tabs, db, t_new,
                              n_pages)
        ksnew_t = jnp.swapaxes(kscale[:, :MLA_HEADS].reshape(db, t_new, MLA_HEADS), 1, 2)
        mix_mla = _mla_sample(pt_l, mla_pages_t, ks_pages_t, q_mla, mlarows, ksnew_t, lw, db, t_new, n_pages)
        sc = state_conv[l]
        zpad = jnp.zeros((db, t_new - 1, 2 * D_FF), F32)
        a1 = jnp.concatenate([sc[:, 1:2], zpad], axis=1)
        a2 = jnp.concatenate([sc, zpad[:, 1:]], axis=1)
        chunked = lambda a: jnp.transpose(a.reshape(ts, 2 * N_FF_CHUNKS, FF_CHUNK), (1, 0, 2))
        xs, tail = _ffn(xs, mix_nsa.astype(BF16), mix_mla.astype(BF16), lw, tm_s, period=t_new,
                        a1=chunked(a1), a2=chunked(a2))
        h_rows = jnp.transpose(tail, (0, 2, 1, 3)).reshape(db, t_new, 2 * D_FF)
        outs_s[0].append(nsarows.reshape(db, t_new, 4, 1, HEAD_DIM))
        outs_s[1].append(mlarows.reshape(db, t_new, KV_LORA + MLA_ROPE))
        outs_s[2].append(kscale[:, :MLA_HEADS].reshape(db, t_new, MLA_HEADS))
        outs_s[3].append(jnp.concatenate([state_win[l][:, t_new:], winrows.reshape(db, t_new, 2, 1, HEAD_DIM)], axis=1))
        outs_s[4].append(h_rows[:, t_new - (CONV_W - 1):])
    stack = lambda o: jnp.stack(o, axis=0)
    nsa_p, mla_p, ks_p, win_p, cv_p = [stack(o) for o in outs_p]
    nsa_s, mla_s, ks_s, win_s, cv_s = [stack(o) for o in outs_s]
    return (xp.reshape(nb, seq, D_MODEL), xs.reshape(db, t_new, D_MODEL), nsa_p, mla_p, ks_p, win_p, cv_p,
            nsa_s, mla_s, ks_s, win_s, cv_s)
```

```python
import functools
import math

import numpy as np
import jax
import jax.numpy as jnp
from jax import lax
from jax.experimental import pallas as pl
from jax.experimental.pallas import tpu as pltpu

F32 = jnp.float32
BF16 = jnp.bfloat16

D_MODEL = 1024
PAGE_SIZE = 128
HEAD_DIM = 64
NSA_HEADS = 8
CMP_STRIDE = 16
CMP_BLOCK = 32
CMP_HIDDEN = 64
SEL_BLOCK = 64
SEL_TOPK = 16
WINDOW = 512
MLA_HEADS = 8
MLA_NOPE = 64
MLA_ROPE = 32
MLA_V = 64
Q_LORA = 384
KV_LORA = 256
ROPE_THETA = 10000.0
D_FF = 2816
CONV_W = 3
REL_BUCKETS = 32
REL_MAX_DIST = 1024
EPS = 1e-6
Q_BLOCK = 128
NSA_SCALE = HEAD_DIM ** -0.5
MLA_SCALE = (MLA_NOPE + MLA_ROPE) ** -0.5

OFF_NSA_KV = NSA_HEADS * HEAD_DIM
OFF_GATE = OFF_NSA_KV + 6 * HEAD_DIM
OFF_QLAT = OFF_GATE + 3 * NSA_HEADS
OFF_CKV = OFF_QLAT + Q_LORA
OFF_KROPE = OFF_CKV + KV_LORA
IN_COLS = OFF_KROPE + MLA_ROPE

LANE = 128
NEG = -1e30
VMEM_LIMIT = 56 * 1024 * 1024

C_Q = 0
C_KV = C_Q + NSA_HEADS * LANE
C_QL = C_KV + 3 * LANE
C_CKV = C_QL + Q_LORA
C_E = C_CKV + KV_LORA
C_END = C_E + LANE

FF_CHUNK = 256
N_FF_CHUNKS = D_FF // FF_CHUNK
PAGES_PER_STEP = 16
SAMPLE_SEQS = 4
SEL_TK = 512
MLA_TQ = 256
MLA_TK = 512


def _dot(a, b):
    return jnp.dot(a, b, preferred_element_type=F32)


def _dot_nt(a, b):
    return lax.dot_general(a, b, (((1,), (1,)), ((), ())), preferred_element_type=F32)


def _params(sem=None):
    return pltpu.CompilerParams(dimension_semantics=sem, vmem_limit_bytes=VMEM_LIMIT)


def _full(shape):
    n = len(shape)
    return pl.BlockSpec(shape, lambda *_: (0,) * n)


def _bucket_thresholds():
    thr = list(range(1, REL_BUCKETS // 2 + 1))
    for k in range(1, REL_BUCKETS // 2):
        d = int(2.0 ** ((3 * k + 32) / 8.0)) - 2
        while d ** 8 < 2 ** (3 * k + 32):
            d += 1
        thr.append(d)
    return thr


_THR = _bucket_thresholds()


def _bucket_np(dist):
    n = np.maximum(np.asarray(dist, np.int64), 0)
    out = np.zeros(n.shape, np.int32)
    for t in _THR:
        out += (n >= t).astype(np.int32)
    return out


def _bias_lookup_kernel(rb_ref, idx_ref, out_ref):
    idx = idx_ref[...]
    vals = [jnp.full(idx.shape, NEG, F32) for _ in range(NSA_HEADS)]
    for k in range(REL_BUCKETS):
        hit = idx == k
        vals = [jnp.where(hit, rb_ref[k, h], vals[h]) for h in range(NSA_HEADS)]
    for h in range(NSA_HEADS):
        out_ref[h] = vals[h]


def _bias_tables(rel_bias, buckets):
    n, t, k = buckets.shape
    out = pl.pallas_call(
        _bias_lookup_kernel,
        grid=(n,),
        in_specs=[pl.BlockSpec(memory_space=pltpu.SMEM), pl.BlockSpec((None, t, k), lambda i: (i, 0, 0))],
        out_specs=pl.BlockSpec((None, NSA_HEADS, t, k), lambda i: (i, 0, 0, 0)),
        out_shape=jax.ShapeDtypeStruct((n, NSA_HEADS, t, k), F32),
        compiler_params=_params(("parallel",)),
        name="bias_lookup",
    )(rel_bias, jnp.asarray(buckets, jnp.int32))
    return out.reshape(n, NSA_HEADS * t, k)


def _masked_bucket(dist, ok):
    return np.where(ok, _bucket_np(dist), REL_BUCKETS).astype(np.int32)


def _proj_kernel(x_ref, gattn_ref, w1_ref, gq_ref, gkv_ref, gql_ref, wqup_ref, gmq_ref, gc_ref,
                 wuk_ref, wuv_ref, ge_ref, rope_ref,
                 qnsa_ref, nsarows_ref, winrows_ref, cmp16_ref, sel16_ref, win16_ref, gates_ref,
                 qmla_ref, mlarows_ref, kscale_ref, kfull_ref, v16_ref):
    tm = x_ref.shape[0]
    lane = lax.broadcasted_iota(jnp.int32, (tm, LANE), 1)
    lo = lane < HEAD_DIM
    x = x_ref[...]
    xn = x * lax.rsqrt(jnp.mean(x * x, axis=-1, keepdims=True) + EPS) * gattn_ref[...]
    xn16 = xn.astype(BF16)

    qb = _dot(xn16, w1_ref[:, C_Q:C_KV])
    for h in range(NSA_HEADS):
        b = qb[:, h * LANE:(h + 1) * LANE]
        rs = lax.rsqrt(jnp.sum(b * b, axis=-1, keepdims=True) * (1.0 / HEAD_DIM) + EPS)
        qnsa_ref[:, h * LANE:(h + 1) * LANE] = (b * rs * gq_ref[...]).astype(qnsa_ref.dtype)

    kv = _dot(xn16, w1_ref[:, C_KV:C_QL])
    b0 = kv[:, 0:LANE]
    nsarows_ref[:, 0:LANE] = b0
    cmp16_ref[...] = b0.astype(BF16)
    for j, (dst32, dst16) in enumerate(((nsarows_ref, sel16_ref), (winrows_ref, win16_ref))):
        b = kv[:, (j + 1) * LANE:(j + 2) * LANE]
        ss = jnp.sum(jnp.where(lo, b * b, 0.0), axis=-1, keepdims=True)
        rs = lax.rsqrt(ss * (1.0 / HEAD_DIM) + EPS)
        bn = b * jnp.where(lo, rs, 1.0) * gkv_ref[j:j + 1, :]
        if j == 0:
            dst32[:, LANE:2 * LANE] = bn
        else:
            dst32[...] = bn
        dst16[...] = bn.astype(BF16)

    e = _dot(xn16, w1_ref[:, C_E:C_END])
    gates_ref[...] = 1.0 / (1.0 + jnp.exp(-e))
    rope_lanes = (lane >= HEAD_DIM) & (lane < HEAD_DIM + MLA_ROPE)
    sse = jnp.sum(jnp.where(rope_lanes, e * e, 0.0), axis=-1, keepdims=True)
    en = e * lax.rsqrt(sse * (1.0 / MLA_ROPE) + EPS) * ge_ref[...]
    krblk = en * rope_ref[:, 2 * LANE:3 * LANE] + pltpu.roll(en * rope_ref[:, 3 * LANE:4 * LANE], LANE - MLA_ROPE, 1)
    mlarows_ref[:, KV_LORA:KV_LORA + MLA_ROPE] = pltpu.roll(krblk, HEAD_DIM, 1)[:, 0:MLA_ROPE]

    ql = _dot(xn16, w1_ref[:, C_QL:C_CKV])
    qln = ql * lax.rsqrt(jnp.mean(ql * ql, axis=-1, keepdims=True) + EPS) * gql_ref[...]
    qa = _dot(qln.astype(BF16), wqup_ref[...])
    ctab = rope_ref[:, 0:LANE]
    stab = rope_ref[:, LANE:2 * LANE]
    for h in range(MLA_HEADS):
        b = qa[:, h * LANE:(h + 1) * LANE]
        b2 = b * b
        ssn = jnp.sum(jnp.where(lo, b2, 0.0), axis=-1, keepdims=True)
        ssr = jnp.sum(jnp.where(rope_lanes, b2, 0.0), axis=-1, keepdims=True)
        scale = jnp.where(lo, lax.rsqrt(ssn * (1.0 / MLA_NOPE) + EPS), lax.rsqrt(ssr * (1.0 / MLA_ROPE) + EPS))
        bn = b * scale * gmq_ref[...]
        q = bn * ctab + pltpu.roll(bn * stab, LANE - MLA_ROPE, 1)
        qmla_ref[:, h * LANE:(h + 1) * LANE] = q.astype(qmla_ref.dtype)

    cb = _dot(xn16, w1_ref[:, C_CKV:C_E])
    c = cb * lax.rsqrt(jnp.mean(cb * cb, axis=-1, keepdims=True) + EPS) * gc_ref[...]
    mlarows_ref[:, 0:KV_LORA] = c
    c16 = c.astype(BF16)
    kn = _dot(c16, wuk_ref[...])
    ksacc = jnp.zeros((tm, LANE), F32)
    for h in range(MLA_HEADS):
        b = kn[:, h * LANE:(h + 1) * LANE]
        ksh = lax.rsqrt(jnp.sum(b * b, axis=-1, keepdims=True) * (1.0 / MLA_NOPE) + EPS)
        ksacc = jnp.where(lane == h, ksh, ksacc)
        kfull_ref[:, h * LANE:(h + 1) * LANE] = (b * ksh + krblk).astype(BF16)
    kscale_ref[...] = ksacc
    v16_ref[...] = _dot(c16, wuv_ref[...]).astype(BF16)


def _proj(x, lw, rope_tab, tm, qdtype):
    t = x.shape[0]
    n_rope = rope_tab.shape[0] // tm
    row = lambda w: pl.BlockSpec((tm, w), lambda i: (i, 0))
    sds = lambda w, dt: jax.ShapeDtypeStruct((t, w), dt)
    in_specs = [row(D_MODEL), _full((1, D_MODEL)), _full((D_MODEL, C_END)), _full((1, LANE)), _full((8, LANE)),
                _full((1, Q_LORA)), _full((Q_LORA, MLA_HEADS * LANE)), _full((1, LANE)), _full((1, KV_LORA)),
                _full((KV_LORA, MLA_HEADS * LANE)), _full((KV_LORA, MLA_HEADS * MLA_V)), _full((1, LANE)),
                pl.BlockSpec((tm, 4 * LANE), lambda i: (i % n_rope, 0))]
    widths = [(NSA_HEADS * LANE, qdtype), (2 * LANE, F32), (LANE, F32), (LANE, BF16), (LANE, BF16), (LANE, BF16),
              (LANE, F32), (MLA_HEADS * LANE, qdtype), (KV_LORA + MLA_ROPE, F32), (LANE, F32),
              (MLA_HEADS * LANE, BF16), (MLA_HEADS * MLA_V, BF16)]
    return pl.pallas_call(
        _proj_kernel,
        grid=(t // tm,),
        in_specs=in_specs,
        out_specs=[row(w) for w, _ in widths],
        out_shape=[sds(w, dt) for w, dt in widths],
        compiler_params=_params(("parallel",)),
        name="proj",
    )(x, lw["gattn"], lw["w1"], lw["gq"], lw["gkv"], lw["gql"], lw["wqup"], lw["gmq"], lw["gc"],
      lw["wuk"], lw["wuv"], lw["ge"], rope_tab)


def _compress_tail(h, pe_r, w2, gk, lane_lo):
    n = h.shape[0]
    pe = pe_r[0:1, 0:LANE] + pe_r[1:2, LANE:2 * LANE]
    hid = jax.nn.gelu(h[:, 0:LANE] + pltpu.roll(h[:, LANE:2 * LANE], n - 1, 0) + pe, approximate=True)
    out = _dot(hid.astype(BF16), w2)
    ss = jnp.sum(jnp.where(lane_lo, out * out, 0.0), axis=-1, keepdims=True)
    return out * jnp.where(lane_lo, lax.rsqrt(ss * (1.0 / HEAD_DIM) + EPS), 1.0) * gk


def _compress_kernel(x_ref, w1_ref, pe_ref, w2_ref, gk_ref, out_ref):
    n = x_ref.shape[0]
    lane_lo = lax.broadcasted_iota(jnp.int32, (n, LANE), 1) < HEAD_DIM
    h = _dot(x_ref[...], w1_ref[...])
    pe_r = _dot(pe_ref[...], w1_ref[...])
    out_ref[...] = _compress_tail(h, pe_r, w2_ref[...], gk_ref[...], lane_lo).astype(BF16)


def _compress(chunks, lw, n_batch):
    n = chunks.shape[0] // n_batch
    kw = CMP_STRIDE * LANE
    return pl.pallas_call(
        _compress_kernel,
        grid=(n_batch,),
        in_specs=[pl.BlockSpec((n, kw), lambda b: (b, 0)), _full((kw, 2 * LANE)), _full((8, kw)),
                  _full((LANE, LANE)), _full((1, LANE))],
        out_specs=pl.BlockSpec((n, LANE), lambda b: (b, 0)),
        out_shape=jax.ShapeDtypeStruct((chunks.shape[0], LANE), BF16),
        compiler_params=_params(("parallel",)),
        name="compress",
    )(chunks, lw["cw1"], lw["cpe"], lw["cw2"], lw["gkc"])


def _topk_mask(score, lane):
    sel = jnp.zeros(score.shape, F32)
    cur = score
    for _ in range(SEL_TOPK):
        pick = lane == jnp.argmax(cur, axis=-1, keepdims=True)
        sel = jnp.where(pick, 1.0, sel)
        cur = jnp.where(pick, -jnp.inf, cur)
    return sel


def _softmax_rows(s, valid):
    m = jnp.max(s, axis=-1, keepdims=True)
    e = jnp.where(valid, jnp.exp(s - m), 0.0)
    return e / jnp.maximum(jnp.sum(e, axis=-1, keepdims=True), 1e-30)


def _nsa_prompt_kernel(q_ref, kvc_ref, sel_ref, win_ref, gates_ref, tbcmp_ref, tbsel_ref, tbwin_ref, map_ref,
                       mix_ref, m_ref, acc_ref):
    qi = pl.program_id(1)
    q0 = qi * Q_BLOCK
    n_cmp = kvc_ref.shape[0]
    qblk = q_ref[...]
    qh = [qblk[:, h * LANE:(h + 1) * LANE] for h in range(NSA_HEADS)]

    kvc = kvc_ref[...]
    cw = min(LANE, n_cmp)
    t_i = lax.broadcasted_iota(jnp.int32, (Q_BLOCK, n_cmp), 0)
    n_i = lax.broadcasted_iota(jnp.int32, (Q_BLOCK, n_cmp), 1)
    valid_c = q0 + t_i - (n_i * CMP_STRIDE + (CMP_BLOCK - 1)) >= 0
    ridx = [jnp.clip(qi - (LANE * CMP_STRIDE // Q_BLOCK) * blk, 0, tbcmp_ref.shape[0] - 1)
            for blk in range(n_cmp // cw)]
    cext = jnp.concatenate([kvc, map_ref[...], jnp.ones((n_cmp, LANE), BF16)], axis=1)
    o_c, imp = [], None
    for h in range(NSA_HEADS):
        r = slice(h * Q_BLOCK, (h + 1) * Q_BLOCK)
        tb = jnp.concatenate([tbcmp_ref[i, r, 0:cw] for i in ridx], axis=1)
        s = jnp.where(valid_c, _dot_nt(qh[h], kvc) + tb, NEG)
        e = jnp.where(valid_c, jnp.exp(s - jnp.max(s, axis=-1, keepdims=True)), 0.0)
        res = _dot(e.astype(BF16), cext)
        inv = 1.0 / jnp.maximum(res[:, 2 * LANE:3 * LANE], 1e-30)
        o_c.append(res[:, 0:LANE] * inv)
        imp = res[:, LANE:2 * LANE] * inv if imp is None else imp + res[:, LANE:2 * LANE] * inv

    n_sel = sel_ref.shape[0] // SEL_BLOCK
    lane = lax.broadcasted_iota(jnp.int32, (Q_BLOCK, LANE), 1)
    qpos = q0 + lax.broadcasted_iota(jnp.int32, (Q_BLOCK, LANE), 0)
    q_blk = qpos // SEL_BLOCK
    forced = (lane == 0) | (lane == q_blk) | (lane == q_blk - 1)
    valid_b = lane * SEL_BLOCK <= qpos
    score = jnp.where(valid_b & (lane < n_sel), jnp.where(forced, jnp.inf, imp), -jnp.inf)
    sel16 = _topk_mask(score, lane).astype(BF16)

    n_w = WINDOW + Q_BLOCK
    kvw = win_ref[pl.ds(pl.multiple_of(q0, Q_BLOCK), n_w), :]
    col_w = lax.broadcasted_iota(jnp.int32, (Q_BLOCK, n_w), 1)
    pos_mask = jnp.where(q0 - WINDOW + col_w >= 0, 0.0, NEG)
    wext = jnp.concatenate([kvw, jnp.ones((n_w, LANE), BF16)], axis=1)
    o_w = []
    for h in range(NSA_HEADS):
        s = _dot_nt(qh[h], kvw) + (tbwin_ref[h * Q_BLOCK:(h + 1) * Q_BLOCK, :] + pos_mask)
        e = jnp.exp(s - jnp.max(s, axis=-1, keepdims=True))
        res = _dot(e.astype(BF16), wext)
        o_w.append(res[:, 0:LANE] / res[:, LANE:2 * LANE])

    m_ref[...] = jnp.full(m_ref.shape, NEG, F32)
    acc_ref[...] = jnp.zeros(acc_ref.shape, F32)
    blk_i = lax.broadcasted_iota(jnp.int32, (LANE, SEL_TK), 0)
    col_i = lax.broadcasted_iota(jnp.int32, (LANE, SEL_TK), 1)
    row_t = lax.broadcasted_iota(jnp.int32, (Q_BLOCK, SEL_TK), 0)
    col_t = lax.broadcasted_iota(jnp.int32, (Q_BLOCK, SEL_TK), 1)
    n_const = tbsel_ref.shape[0] - 1
    ones = jnp.ones((SEL_TK, LANE), BF16)

    def sel_step(kt, carry):
        k0 = pl.multiple_of(kt * SEL_TK, SEL_TK)
        kv = sel_ref[pl.ds(k0, SEL_TK), :]
        kvext = jnp.concatenate([kv, ones], axis=1)
        expand = jnp.where(blk_i == (k0 + col_i) // SEL_BLOCK, 1.0, 0.0).astype(BF16)
        mk = _dot(sel16, expand)
        add_mask = jnp.where((mk > 0.5) & (k0 + col_t <= q0 + row_t), 0.0, NEG)
        didx = [jnp.clip((q0 - k0) // LANE - j, 0, n_const) for j in range(SEL_TK // LANE)]
        ss = [_dot_nt(qh[h], kv) for h in range(NSA_HEADS)]
        ps, alphas = [], []
        for h in range(NSA_HEADS):
            r = slice(h * Q_BLOCK, (h + 1) * Q_BLOCK)
            tb = jnp.concatenate([tbsel_ref[d, r, :] for d in didx], axis=1)
            s = ss[h] + (tb + add_mask)
            m_old = m_ref[h]
            m_new = jnp.maximum(m_old, jnp.max(s, axis=-1, keepdims=True))
            alphas.append(jnp.exp(m_old - m_new))
            ps.append(jnp.exp(s - jnp.concatenate([m_new] * (SEL_TK // LANE), axis=1)).astype(BF16))
            m_ref[h] = m_new
        for h in range(NSA_HEADS):
            acc_ref[h] = jnp.concatenate([alphas[h]] * 2, axis=1) * acc_ref[h] + _dot(ps[h], kvext)
        return carry

    lax.fori_loop(0, q0 // SEL_TK + 1, sel_step, 0)

    g = gates_ref[...]
    mixed = []
    for h in range(NSA_HEADS):
        r = slice(h * Q_BLOCK, (h + 1) * Q_BLOCK)
        o_s = acc_ref[h, :, 0:LANE] / acc_ref[h, :, LANE:2 * LANE]
        mixed.append(g[:, h:h + 1] * o_c[h] + g[:, NSA_HEADS + h:NSA_HEADS + h + 1] * o_s
                     + g[:, 2 * NSA_HEADS + h:2 * NSA_HEADS + h + 1] * o_w[h])
    for j in range(NSA_HEADS // 2):
        slab = jnp.where(lane < HEAD_DIM, pltpu.roll(mixed[2 * j], HEAD_DIM, 1), mixed[2 * j + 1])
        mix_ref[:, j * LANE:(j + 1) * LANE] = slab.astype(BF16)


def _nsa_prompt(q, kvc, sel16, win16p, gates, tbcmp, tbsel, tbwin, cmap, n_batch, seq):
    nqb = seq // Q_BLOCK
    n_cmp = seq // CMP_STRIDE
    const = lambda a: pl.BlockSpec(a.shape, lambda b, i: (0,) * a.ndim, pipeline_mode=pl.Buffered(1))
    return pl.pallas_call(
        _nsa_prompt_kernel,
        grid=(n_batch, nqb),
        in_specs=[pl.BlockSpec((Q_BLOCK, NSA_HEADS * LANE), lambda b, i: (b * nqb + i, 0)),
                  pl.BlockSpec((n_cmp, LANE), lambda b, i: (b, 0)),
                  pl.BlockSpec((seq, LANE), lambda b, i: (b, 0)),
                  pl.BlockSpec((None, seq + WINDOW, LANE), lambda b, i: (b, 0, 0)),
                  pl.BlockSpec((Q_BLOCK, LANE), lambda b, i: (b * nqb + i, 0)),
                  const(tbcmp), const(tbsel), const(tbwin), const(cmap)],
        out_specs=pl.BlockSpec((Q_BLOCK, NSA_HEADS * HEAD_DIM), lambda b, i: (b * nqb + i, 0)),
        scratch_shapes=[pltpu.VMEM((NSA_HEADS, Q_BLOCK, LANE), F32), pltpu.VMEM((NSA_HEADS, Q_BLOCK, 2 * LANE), F32)],
        out_shape=jax.ShapeDtypeStruct((n_batch * seq, NSA_HEADS * HEAD_DIM), BF16),
        compiler_params=_params(("parallel", "parallel")),
        name="nsa_prompt",
    )(q, kvc, sel16, win16p, gates, tbcmp, tbsel, tbwin, cmap)


def _mla_prompt_kernel(q_ref, k_ref, v_ref, o_ref, m_ref, acc_ref):
    qi = pl.program_id(1)
    tq = q_ref.shape[0]
    tk = MLA_TK
    q0 = qi * tq
    lane = lax.broadcasted_iota(jnp.int32, (tq, LANE), 1)
    m_ref[...] = jnp.full(m_ref.shape, NEG, F32)
    acc_ref[...] = jnp.zeros(acc_ref.shape, F32)
    ones = jnp.ones((tk, LANE), BF16)

    def step(kt, mask):
        k0 = pl.multiple_of(kt * tk, tk)
        ss = [_dot_nt(q_ref[:, h * LANE:(h + 1) * LANE], k_ref[pl.ds(k0, tk), h * LANE:(h + 1) * LANE])
              for h in range(MLA_HEADS)]
        ps, alphas = [], []
        for h in range(MLA_HEADS):
            s = ss[h] if mask is None else ss[h] + mask
            m_old = m_ref[h]
            m_new = jnp.maximum(m_old, jnp.max(s, axis=-1, keepdims=True))
            alphas.append(jnp.exp(m_old - m_new))
            ps.append(jnp.exp(s - jnp.concatenate([m_new] * (tk // LANE), axis=1)).astype(BF16))
            m_ref[h] = m_new
        for h in range(MLA_HEADS):
            vl = (h // 2) * LANE
            vext = jnp.concatenate([v_ref[pl.ds(k0, tk), vl:vl + LANE], ones], axis=1)
            acc_ref[h] = jnp.concatenate([alphas[h]] * 2, axis=1) * acc_ref[h] + _dot(ps[h], vext)

    def body(kt, carry):
        step(kt, None)
        return carry

    n_full = q0 // tk
    lax.fori_loop(0, n_full, body, 0)
    row_i = lax.broadcasted_iota(jnp.int32, (tq, tk), 0)
    col_i = lax.broadcasted_iota(jnp.int32, (tq, tk), 1)
    step(n_full, jnp.where(n_full * tk + col_i <= q0 + row_i, 0.0, NEG))
    for j in range(MLA_HEADS // 2):
        lo = acc_ref[2 * j, :, 0:LANE] / acc_ref[2 * j, :, LANE:2 * LANE]
        hi = acc_ref[2 * j + 1, :, 0:LANE] / acc_ref[2 * j + 1, :, LANE:2 * LANE]
        o_ref[:, j * LANE:(j + 1) * LANE] = jnp.where(lane < MLA_V, lo, hi).astype(BF16)


def _mla_prompt(q, kfull, v16, n_batch, seq, tq):
    nq = seq // tq
    return pl.pallas_call(
        _mla_prompt_kernel,
        grid=(n_batch, nq),
        in_specs=[pl.BlockSpec((tq, MLA_HEADS * LANE), lambda b, i: (b * nq + i, 0)),
                  pl.BlockSpec((seq, MLA_HEADS * LANE), lambda b, i: (b, 0)),
                  pl.BlockSpec((seq, MLA_HEADS * MLA_V), lambda b, i: (b, 0))],
        out_specs=pl.BlockSpec((tq, MLA_HEADS * MLA_V), lambda b, i: (b * nq + i, 0)),
        out_shape=jax.ShapeDtypeStruct((n_batch * seq, MLA_HEADS * MLA_V), BF16),
        scratch_shapes=[pltpu.VMEM((MLA_HEADS, tq, LANE), F32), pltpu.VMEM((MLA_HEADS, tq, 2 * LANE), F32)],
        compiler_params=_params(("parallel", "parallel")),
        name="mla_prompt",
    )(q, kfull, v16)


def _ffn_kernel(x_ref, mixa_ref, mixb_ref, wo_ref, gffn_ref, wup_ref, cw_ref, wdn_ref, *rest, period, tiles_per_seq):
    if period is None:
        y_ref, tail_ref, hn_ref, h_ref, acc_ref, bufu_ref, bufg_ref, carry_ref = rest
    else:
        a1_ref, a2_ref, y_ref, tail_ref, hn_ref, h_ref, acc_ref, bufu_ref, bufg_ref = rest
    tm = x_ref.shape[0]
    half = wo_ref.shape[0] // 2
    h = x_ref[...] + _dot(mixa_ref[...], wo_ref[0:half, :]) + _dot(mixb_ref[...], wo_ref[half:, :])
    h_ref[...] = h
    hn = h * lax.rsqrt(jnp.mean(h * h, axis=-1, keepdims=True) + EPS) * gffn_ref[...]
    hn_ref[...] = hn.astype(BF16)
    acc_ref[...] = jnp.zeros(acc_ref.shape, F32)
    if period is None:
        first = pl.program_id(0) % tiles_per_seq == 0
    else:
        tmod = lax.broadcasted_iota(jnp.int32, (tm, FF_CHUNK), 0) % period

    def chunk(c, carry):
        hn16 = hn_ref[...]
        convd = []
        for part, buf in ((0, bufu_ref), (1, bufg_ref)):
            idx = part * N_FF_CHUNKS + c
            hc = _dot(hn16, wup_ref[idx])
            buf[8:8 + tm, :] = hc
            tail_ref[idx] = hc[tm - tail_ref.shape[1]:tm, :]
            if period is None:
                prev = carry_ref[idx]
                buf[0:8, :] = jnp.where(first, 0.0, prev)
                carry_ref[idx] = hc[tm - 8:tm, :]
                h1 = buf[7:7 + tm, :]
                h2 = buf[6:6 + tm, :]
            else:
                buf[0:8, :] = jnp.zeros((8, FF_CHUNK), F32)
                h1 = jnp.where(tmod >= 1, buf[7:7 + tm, :], a1_ref[idx])
                h2 = jnp.where(tmod >= 2, buf[6:6 + tm, :], a2_ref[idx])
            cw = cw_ref[idx]
            convd.append(cw[3:4, :] + ((h2 * cw[0:1, :] + h1 * cw[1:2, :]) + hc * cw[2:3, :]))
        u, g = convd
        act = (g * (1.0 / (1.0 + jnp.exp(-g)))) * u
        acc_ref[...] += _dot(act.astype(BF16), wdn_ref[c])
        return carry

    lax.fori_loop(0, N_FF_CHUNKS, chunk, 0)
    y_ref[...] = h_ref[...] + acc_ref[...]


def _ffn(x, mixa, mixb, lw, tm, tiles_per_seq=None, period=None, a1=None, a2=None):
    t = x.shape[0]
    n_tiles = t // tm
    row = lambda w: pl.BlockSpec((tm, w), lambda i: (i, 0))
    once = pl.Buffered(1)
    const = lambda shape: pl.BlockSpec(shape, lambda i: (0,) * len(shape), pipeline_mode=once)
    in_specs = [row(D_MODEL), row(NSA_HEADS * HEAD_DIM), row(MLA_HEADS * MLA_V),
                const((D_MODEL, D_MODEL)), const((1, D_MODEL)),
                const((2 * N_FF_CHUNKS, D_MODEL, FF_CHUNK)), const((2 * N_FF_CHUNKS, 8, FF_CHUNK)),
                const((N_FF_CHUNKS, FF_CHUNK, D_MODEL))]
    args = [x, mixa, mixb, lw["wo"], lw["gffn"], lw["wup"], lw["cw"], lw["wdn"]]
    tail_rows = 8 if period is None else tm
    scratch = [pltpu.VMEM((tm, D_MODEL), BF16), pltpu.VMEM((tm, D_MODEL), F32), pltpu.VMEM((tm, D_MODEL), F32),
               pltpu.VMEM((tm + 8, FF_CHUNK), F32), pltpu.VMEM((tm + 8, FF_CHUNK), F32)]
    if period is None:
        scratch.append(pltpu.VMEM((2 * N_FF_CHUNKS, 8, FF_CHUNK), F32))
    else:
        in_specs += [pl.BlockSpec((2 * N_FF_CHUNKS, tm, FF_CHUNK), lambda i: (0, i, 0))] * 2
        args += [a1, a2]
    return pl.pallas_call(
        functools.partial(_ffn_kernel, period=period, tiles_per_seq=tiles_per_seq),
        grid=(n_tiles,),
        in_specs=in_specs,
        out_specs=[row(D_MODEL), pl.BlockSpec((None, 2 * N_FF_CHUNKS, tail_rows, FF_CHUNK), lambda i: (i, 0, 0, 0))],
        out_shape=[jax.ShapeDtypeStruct((t, D_MODEL), F32),
                   jax.ShapeDtypeStruct((n_tiles, 2 * N_FF_CHUNKS, tail_rows, FF_CHUNK), F32)],
        scratch_shapes=scratch,
        compiler_params=_params(("arbitrary",)),
        name="ffn",
    )(*args)


def _tree_sum(xs):
    while len(xs) > 1:
        xs = [xs[i] + xs[i + 1] for i in range(0, len(xs) - 1, 2)] + ([xs[-1]] if len(xs) % 2 else [])
    return xs[0]


def _round_robin(gens):
    live = list(gens)
    while live:
        nxt = []
        for gen in live:
            try:
                next(gen)
                nxt.append(gen)
            except StopIteration:
                pass
        live = nxt


def _nsa_sample_kernel(pt_ref, *refs, n_seq):
    del pt_ref
    pg = PAGES_PER_STEP
    pages = refs[:n_seq * pg]
    (q_ref, rows_ref, winnew_ref, state_ref, gates_ref, w1_ref, pe_ref, w2_ref, gk_ref, map_ref, tbc_ref, tbs_ref,
     tbw_ref, place_ref, expand_ref, mix_ref, hs_ref, sel_ref, stage_ref, qs_ref, s_ref) = refs[n_seq * pg:]
    j = pl.program_id(1)
    n_groups = pl.num_programs(1)
    n_pages = sel_ref.shape[1] - 1
    past = n_pages * PAGE_SIZE
    t_new = q_ref.shape[0] // n_seq
    chunks_per_page = PAGE_SIZE // CMP_STRIDE
    step_keys = pg * PAGE_SIZE

    @pl.when(j == 0)
    def _():
        for g in range(n_seq):
            qblk = q_ref[g * t_new:(g + 1) * t_new, :]
            qs_ref[g] = jnp.concatenate([qblk[:, h * LANE:(h + 1) * LANE] for h in range(NSA_HEADS)],
                                        axis=0).astype(BF16)

    hbase = pl.multiple_of(j * (pg * chunks_per_page), pg * chunks_per_page)
    kbase = pl.multiple_of(j * step_keys, step_keys)
    for g in range(n_seq):
        xs = []
        for k in range(pg):
            page = pages[g * pg + k]
            stage_ref[g * pg + k] = jnp.transpose(page[0:LANE, :])
            xs.append(jnp.concatenate([stage_ref[g * pg + k, pl.ds(r, chunks_per_page, stride=CMP_STRIDE), :]
                                       for r in range(CMP_STRIDE)], axis=1))
            sel_ref[g, j * pg + k] = page[LANE:2 * LANE, :].astype(BF16)
        x = jnp.concatenate(xs, axis=0).astype(BF16)
        hs_ref[g, pl.ds(hbase, pg * chunks_per_page), :] = _dot(x, w1_ref[...])
        kvt_step = jnp.concatenate([sel_ref[g, j * pg + k] for k in range(pg)], axis=1)
        s_ref[g, :, pl.ds(kbase, step_keys)] = _dot(qs_ref[g], kvt_step) + tbs_ref[:, pl.ds(kbase, step_keys)]

    def tail(g):
        tok = slice(g * t_new, (g + 1) * t_new)
        n_cmp = hs_ref.shape[1]
        lane_c = lax.broadcasted_iota(jnp.int32, (n_cmp, LANE), 1) < HEAD_DIM
        pe_r = _dot(pe_ref[...], w1_ref[...])
        kvc = _compress_tail(hs_ref[g], pe_r, w2_ref[...], gk_ref[...], lane_c).astype(BF16)
        qs = qs_ref[g]
        yield

        tbc = tbc_ref[...]
        valid_c = tbc > 0.5 * NEG
        sc = jnp.where(valid_c, _dot_nt(qs, kvc) + tbc, NEG)
        ec = jnp.where(valid_c, jnp.exp(sc - jnp.max(sc, axis=-1, keepdims=True)), 0.0).astype(BF16)
        yield
        n_lane = map_ref.shape[1]
        resc = _dot(ec, jnp.concatenate([kvc, map_ref[...], jnp.ones((n_cmp, LANE), BF16)], axis=1))
        invc = 1.0 / jnp.maximum(resc[:, LANE + n_lane:], 1e-30)
        o_c = resc[:, 0:LANE] * invc
        imp_h = resc[:, LANE:LANE + n_lane] * jnp.concatenate([invc] * (n_lane // LANE), axis=1)
        imp = _tree_sum([imp_h[h * t_new:(h + 1) * t_new, :] for h in range(NSA_HEADS)])

        n_sel = past // SEL_BLOCK + 1
        lane = lax.broadcasted_iota(jnp.int32, (t_new, n_lane), 1)
        qpos = past + lax.broadcasted_iota(jnp.int32, (t_new, n_lane), 0)
        q_blk = qpos // SEL_BLOCK
        forced = (lane == 0) | (lane == q_blk) | (lane == q_blk - 1)
        valid_b = lane * SEL_BLOCK <= qpos
        score = jnp.where(valid_b & (lane < n_sel), jnp.where(forced, jnp.inf, imp), -jnp.inf)
        sel = jnp.zeros(score.shape, F32)
        for _ in range(SEL_TOPK):
            yield
            pick = lane == jnp.argmax(score, axis=-1, keepdims=True)
            sel = jnp.where(pick, 1.0, sel)
            score = jnp.where(pick, -jnp.inf, score)
        sel16 = jnp.concatenate([sel] * NSA_HEADS, axis=0).astype(BF16)

        pad = jnp.zeros((PAGE_SIZE - t_new, LANE), F32)
        newt = jnp.transpose(jnp.concatenate([rows_ref[tok, LANE:2 * LANE], pad], axis=0)).astype(BF16)
        sel_ref[g, n_pages] = newt
        s_ref[g, :, past:past + PAGE_SIZE] = _dot(qs, newt) + tbs_ref[:, past:past + PAGE_SIZE]
        ck = 8 * PAGE_SIZE
        chunks = [(c * ck, ck) for c in range(past // ck)] + [(past, PAGE_SIZE)]
        parts = []
        for k0, w in chunks:
            yield
            hit = _dot(sel16, expand_ref[:, k0:k0 + w])
            s = s_ref[g, :, k0:k0 + w] + jnp.where(hit > 0.5, 0.0, NEG)
            s_ref[g, :, k0:k0 + w] = s
            parts.append(functools.reduce(jnp.maximum, [s[:, i * LANE:(i + 1) * LANE] for i in range(w // LANE)]))
        m_s = jnp.max(functools.reduce(jnp.maximum, parts), axis=-1, keepdims=True)
        accs = []
        for k0, w in chunks:
            yield
            p16 = jnp.exp(s_ref[g, :, k0:k0 + w] - m_s).astype(BF16)
            kvt = jnp.concatenate([sel_ref[g, k0 // PAGE_SIZE + i] for i in range(w // PAGE_SIZE)], axis=1)
            accs.append(_dot_nt(p16, jnp.concatenate([kvt, jnp.ones((LANE, w), BF16)], axis=0)))
        acc = _tree_sum(accs)
        o_s = acc[:, 0:LANE] / acc[:, LANE:2 * LANE]
        yield

        newwt = jnp.transpose(jnp.concatenate([winnew_ref[tok, :], pad], axis=0))
        kvwt = jnp.concatenate([state_ref[g], newwt], axis=1).astype(BF16)
        sw = _dot(qs, kvwt) + tbw_ref[...]
        ew = jnp.exp(sw - jnp.max(sw, axis=-1, keepdims=True)).astype(BF16)
        yield
        rw = _dot_nt(ew, jnp.concatenate([kvwt, jnp.ones((LANE, kvwt.shape[1]), BF16)], axis=0))
        o_w = rw[:, 0:LANE] / rw[:, LANE:2 * LANE]
        yield

        gt = gates_ref[tok, :]
        mixed = []
        for h in range(NSA_HEADS):
            r = slice(h * t_new, (h + 1) * t_new)
            mixed.append((gt[:, h:h + 1] * o_c[r] + gt[:, NSA_HEADS + h:NSA_HEADS + h + 1] * o_s[r]
                          + gt[:, 2 * NSA_HEADS + h:2 * NSA_HEADS + h + 1] * o_w[r]).astype(BF16))
        mix_ref[tok, :] = _dot(jnp.concatenate(mixed, axis=1), place_ref[...])

    @pl.when(j == n_groups - 1)
    def _():
        _round_robin([tail(g) for g in range(n_seq)])


def _nsa_sample(pt_l, pages_t, q, nsarows, winrows, state_t, gates, lw, tabs, n_batch, t_new, n_pages):
    pg = PAGES_PER_STEP
    ns = SAMPLE_SEQS if n_batch % SAMPLE_SEQS == 0 else 1
    n_groups = n_pages // pg
    rows = NSA_HEADS * t_new
    n_cmp = n_pages * PAGE_SIZE // CMP_STRIDE

    def page_spec(g, k):
        return pl.BlockSpec((None, 4 * HEAD_DIM, PAGE_SIZE), lambda b, j, pt: (pt[b * ns + g, j * pg + k], 0, 0))

    per_b = lambda w: pl.BlockSpec((ns * t_new, w), lambda b, j, pt: (b, 0))
    const = lambda a: pl.BlockSpec(a.shape, lambda b, j, pt: (0,) * a.ndim, pipeline_mode=pl.Buffered(1))
    consts = [lw["cw1"], lw["cpe"], lw["cw2"], lw["gkc"], tabs["map_s"], tabs["tbc_s"], tabs["tbs_s"], tabs["tbw_s"],
              tabs["place"], tabs["expand_s"]]
    grid_spec = pltpu.PrefetchScalarGridSpec(
        num_scalar_prefetch=1,
        grid=(n_batch // ns, n_groups),
        in_specs=[page_spec(g, k) for g in range(ns) for k in range(pg)]
        + [per_b(NSA_HEADS * LANE), per_b(2 * LANE), per_b(LANE),
           pl.BlockSpec((ns, LANE, state_t.shape[2]), lambda b, j, pt: (b, 0, 0)), per_b(LANE)]
        + [const(a) for a in consts],
        out_specs=per_b(NSA_HEADS * HEAD_DIM),
        scratch_shapes=[pltpu.VMEM((ns, n_cmp, 2 * LANE), F32), pltpu.VMEM((ns, n_pages + 1, LANE, PAGE_SIZE), BF16),
                        pltpu.VMEM((ns * pg, PAGE_SIZE, LANE), F32), pltpu.VMEM((ns, rows, LANE), BF16),
                        pltpu.VMEM((ns, rows, (n_pages + 1) * PAGE_SIZE), F32)],
    )
    return pl.pallas_call(
        functools.partial(_nsa_sample_kernel, n_seq=ns),
        grid_spec=grid_spec,
        out_shape=jax.ShapeDtypeStruct((n_batch * t_new, NSA_HEADS * HEAD_DIM), F32),
        compiler_params=_params(("parallel", "arbitrary")),
        name="nsa_sample",
    )(pt_l, *([pages_t] * (ns * pg)), q, nsarows, winrows, state_t, gates, *consts)


def _mla_sample_kernel(pt_ref, *refs, n_seq):
    del pt_ref
    pg = PAGES_PER_STEP
    pages = refs[:n_seq * pg]
    kss = refs[n_seq * pg:2 * n_seq * pg]
    (q_ref, new_ref, ksnew_ref, wukt_ref, wuvp_ref, o_ref, qa_ref, qr_ref, m_ref, l_ref, acc_ref) = refs[2 * n_seq * pg:]
    j = pl.program_id(1)
    n_groups = pl.num_programs(1)
    t_new = q_ref.shape[0] // n_seq
    rows = MLA_HEADS * t_new

    @pl.when(j == 0)
    def _():
        lane = lax.broadcasted_iota(jnp.int32, (t_new, LANE), 1)
        rope_lanes = (lane >= MLA_NOPE) & (lane < MLA_NOPE + MLA_ROPE)
        for g in range(n_seq):
            qblk = q_ref[g * t_new:(g + 1) * t_new, :]
            for h in range(MLA_HEADS):
                qh = qblk[:, h * LANE:(h + 1) * LANE]
                qa_ref[g, h * t_new:(h + 1) * t_new, :] = _dot(qh.astype(BF16), wukt_ref[h]).astype(BF16)
                qr = pltpu.roll(jnp.where(rope_lanes, qh, 0.0), LANE - MLA_NOPE, 1)
                qr_ref[g, h * t_new:(h + 1) * t_new, :] = qr[:, 0:MLA_ROPE].astype(BF16)
        m_ref[...] = jnp.full(m_ref.shape, NEG, F32)
        l_ref[...] = jnp.zeros(l_ref.shape, F32)
        acc_ref[...] = jnp.zeros(acc_ref.shape, F32)

    def update(g, s, pv):
        m_old = m_ref[g]
        m_new = jnp.maximum(m_old, jnp.max(s, axis=-1, keepdims=True))
        alpha = jnp.exp(m_old - m_new)
        p = jnp.exp(s - m_new)
        l_ref[g] = alpha * l_ref[g] + jnp.sum(p, axis=-1, keepdims=True)
        m_ref[g] = m_new
        yield
        acc_ref[g] = alpha * acc_ref[g] + pv(p.astype(BF16))

    def head_rows(ks):
        return jnp.concatenate([jnp.broadcast_to(ks[h:h + 1, :], (t_new, ks.shape[1])) for h in range(MLA_HEADS)],
                               axis=0)

    def step(g):
        ct = jnp.concatenate([pages[g * pg + k][...] for k in range(pg)], axis=1).astype(BF16)
        ks = jnp.concatenate([kss[g * pg + k][...] for k in range(pg)], axis=1)
        ct_lat = ct[0:KV_LORA, :]
        yield
        s = _dot(qa_ref[g], ct_lat) * head_rows(ks) + _dot(qr_ref[g], ct[KV_LORA:, :])
        yield
        yield from update(g, s, lambda p16: _dot_nt(p16, ct_lat))

    _round_robin([step(g) for g in range(n_seq)])

    def tail(g):
        tok = slice(g * t_new, (g + 1) * t_new)
        new16 = new_ref[tok, :].astype(BF16)
        new_lat = new16[:, 0:KV_LORA]
        r_t = lax.broadcasted_iota(jnp.int32, (rows, t_new), 0) % t_new
        c_t = lax.broadcasted_iota(jnp.int32, (rows, t_new), 1)
        s_new = (_dot_nt(qa_ref[g], new_lat) * head_rows(ksnew_ref[g]) + _dot_nt(qr_ref[g], new16[:, KV_LORA:]))
        yield
        yield from update(g, s_new + jnp.where(c_t <= r_t, 0.0, NEG), lambda p16: _dot(p16, new_lat))
        yield
        lat16 = (acc_ref[g] / l_ref[g]).astype(BF16)
        lat_wide = jnp.concatenate([lat16[h * t_new:(h + 1) * t_new, :] for h in range(MLA_HEADS)], axis=1)
        o_ref[tok, :] = _dot(lat_wide, wuvp_ref[...])

    @pl.when(j == n_groups - 1)
    def _():
        _round_robin([tail(g) for g in range(n_seq)])


def _mla_sample(pt_l, cache_pages, cache_ks_t, q, mlarows, ksnew_t, lw, n_batch, t_new, n_pages):
    pg = PAGES_PER_STEP
    ns = SAMPLE_SEQS if n_batch % SAMPLE_SEQS == 0 else 1
    n_groups = n_pages // pg
    rows = MLA_HEADS * t_new
    width = KV_LORA + MLA_ROPE

    def page_spec(g, k):
        return pl.BlockSpec((None, width, PAGE_SIZE), lambda b, j, pt: (pt[b * ns + g, j * pg + k], 0, 0))

    def ks_spec(g, k):
        return pl.BlockSpec((None, MLA_HEADS, PAGE_SIZE), lambda b, j, pt: (pt[b * ns + g, j * pg + k], 0, 0))

    per_b = lambda w: pl.BlockSpec((ns * t_new, w), lambda b, j, pt: (b, 0))
    const = lambda a: pl.BlockSpec(a.shape, lambda b, j, pt: (0,) * a.ndim, pipeline_mode=pl.Buffered(1))
    wuvp = lw["wuvp"].reshape(MLA_HEADS * KV_LORA, MLA_HEADS * MLA_V)
    grid_spec = pltpu.PrefetchScalarGridSpec(
        num_scalar_prefetch=1,
        grid=(n_batch // ns, n_groups),
        in_specs=[page_spec(g, k) for g in range(ns) for k in range(pg)]
        + [ks_spec(g, k) for g in range(ns) for k in range(pg)]
        + [per_b(MLA_HEADS * LANE), per_b(width), pl.BlockSpec((ns, MLA_HEADS, t_new), lambda b, j, pt: (b, 0, 0)),
           const(lw["wukt"]), const(wuvp)],
        out_specs=per_b(MLA_HEADS * MLA_V),
        scratch_shapes=[pltpu.VMEM((ns, rows, KV_LORA), BF16), pltpu.VMEM((ns, rows, MLA_ROPE), BF16),
                        pltpu.VMEM((ns, rows, 1), F32), pltpu.VMEM((ns, rows, 1), F32),
                        pltpu.VMEM((ns, rows, KV_LORA), F32)],
    )
    return pl.pallas_call(
        functools.partial(_mla_sample_kernel, n_seq=ns),
        grid_spec=grid_spec,
        out_shape=jax.ShapeDtypeStruct((n_batch * t_new, MLA_HEADS * MLA_V), F32),
        compiler_params=_params(("parallel", "arbitrary")),
        name="mla_sample",
    )(pt_l, *([cache_pages] * (ns * pg)), *([cache_ks_t] * (ns * pg)), q, mlarows, ksnew_t, lw["wukt"], wuvp)


def _swap_halves(a):
    half = a.shape[-1] // 2
    return jnp.concatenate([a[..., half:], a[..., :half]], axis=-1)


def _layer_weights(p, l):
    w_in = p["w_in"][l]
    zeros = lambda *s: jnp.zeros(s, F32)
    q_cols = jnp.concatenate([w_in[:, :OFF_NSA_KV].reshape(D_MODEL, NSA_HEADS, HEAD_DIM),
                              zeros(D_MODEL, NSA_HEADS, LANE - HEAD_DIM)], axis=-1).reshape(D_MODEL, -1)
    k_rope = w_in[:, OFF_KROPE:IN_COLS]
    e_cols = jnp.concatenate([w_in[:, OFF_GATE:OFF_QLAT], zeros(D_MODEL, HEAD_DIM - 3 * NSA_HEADS),
                              k_rope, _swap_halves(k_rope)], axis=-1)
    w1 = jnp.concatenate([q_cols, w_in[:, OFF_NSA_KV:OFF_GATE], w_in[:, OFF_QLAT:OFF_CKV],
                          w_in[:, OFF_CKV:OFF_KROPE], e_cols], axis=-1).astype(BF16)
    nn = p["nsa_norm"][l]
    ones64 = jnp.ones((HEAD_DIM,), F32)
    gq = jnp.concatenate([nn[0] * NSA_SCALE, zeros(HEAD_DIM)])[None]
    gkv = jnp.concatenate([jnp.stack([jnp.concatenate([nn[2], ones64]), jnp.concatenate([nn[3], ones64])]),
                           zeros(6, LANE)], axis=0)
    qu = p["w_q_up"][l].reshape(Q_LORA, MLA_HEADS, MLA_NOPE + MLA_ROPE)
    wqup = jnp.concatenate([qu, _swap_halves(qu[..., MLA_NOPE:])], axis=-1).reshape(Q_LORA, -1).astype(BF16)
    gn, gr = p["mla_nope_norm"][l], p["mla_rope_norm"][l]
    gmq = jnp.concatenate([gn[0] * gn[1], gr[0], _swap_halves(gr[0])])[None]
    ge = jnp.concatenate([zeros(HEAD_DIM), gr[1], _swap_halves(gr[1])])[None]
    wuk = jnp.concatenate([p["w_uk"][l], zeros(KV_LORA, MLA_HEADS, LANE - MLA_NOPE)], axis=-1)
    wuk = wuk.reshape(KV_LORA, -1).astype(BF16)
    wuv = p["w_uv"][l].reshape(KV_LORA, -1).astype(BF16)
    wukt = jnp.concatenate([jnp.transpose(p["w_uk"][l], (1, 2, 0)), zeros(MLA_HEADS, LANE - MLA_NOPE, KV_LORA)],
                           axis=1).astype(BF16)
    eye = jnp.eye(MLA_HEADS, dtype=F32)
    wuvp = (jnp.transpose(p["w_uv"][l], (1, 0, 2))[:, :, None, :] * eye[:, None, :, None]).reshape(
        MLA_HEADS, KV_LORA, MLA_HEADS * MLA_V).astype(BF16)
    w1r = p["cmp_w1"][l].reshape(2, 2, CMP_STRIDE, HEAD_DIM, CMP_HIDDEN)
    z = zeros(CMP_STRIDE, HEAD_DIM, CMP_HIDDEN)
    k_rows = jnp.concatenate([w1r[0, 0], z, w1r[0, 1], z], axis=-1)
    v_rows = jnp.concatenate([z, w1r[1, 0], z, w1r[1, 1]], axis=-1)
    cw1 = jnp.concatenate([k_rows, v_rows], axis=1).reshape(CMP_STRIDE * LANE, 2 * LANE).astype(BF16)
    pe = p["cmp_pe"][l].reshape(2, 2, CMP_STRIDE, HEAD_DIM)
    pe_rows = jnp.concatenate([pe[0], pe[1]], axis=-1).reshape(2, CMP_STRIDE * LANE)
    cpe = jnp.concatenate([pe_rows, zeros(6, CMP_STRIDE * LANE)], axis=0).astype(BF16)
    w2 = p["cmp_w2"][l]
    z2 = zeros(CMP_HIDDEN, HEAD_DIM)
    cw2 = jnp.concatenate([jnp.concatenate([w2[0], z2], axis=1), jnp.concatenate([z2, w2[1]], axis=1)],
                          axis=0).astype(BF16)
    gkc = jnp.concatenate([nn[1], ones64])[None]
    wup = jnp.transpose(p["w_up"][l].reshape(D_MODEL, 2 * N_FF_CHUNKS, FF_CHUNK), (1, 0, 2)).astype(BF16)
    cwb = jnp.concatenate([p["conv_w"][l], p["conv_b"][l][None], zeros(8 - CONV_W - 1, 2 * D_FF)], axis=0)
    cw = jnp.transpose(cwb.reshape(8, 2 * N_FF_CHUNKS, FF_CHUNK), (1, 0, 2))
    wdn = p["w_down"][l].reshape(N_FF_CHUNKS, FF_CHUNK, D_MODEL).astype(BF16)
    return dict(gattn=p["attn_norm"][l][None], w1=w1, gq=gq, gkv=gkv, gql=p["mla_q_lat_norm"][l][None],
                wqup=wqup, gmq=gmq, gc=p["mla_kv_norm"][l][None], wuk=wuk, wuv=wuv, ge=ge, wukt=wukt, wuvp=wuvp,
                cw1=cw1, cpe=cpe, cw2=cw2, gkc=gkc, wo=p["w_o"][l].astype(BF16), gffn=p["ffn_norm"][l][None],
                wup=wup, cw=cw, wdn=wdn)


def _rope_table(pos):
    half = MLA_ROPE // 2
    inv_freq = ROPE_THETA ** (-jnp.arange(half, dtype=F32) / half)
    ang = pos.astype(F32)[:, None] * inv_freq[None, :]
    cos, sin = jnp.cos(ang), jnp.sin(ang)
    n = pos.shape[0]
    z = lambda w: jnp.zeros((n, w), F32)
    cc = jnp.concatenate([cos, cos], axis=1)
    ss = jnp.concatenate([-sin, sin], axis=1)
    ctab = jnp.concatenate([jnp.full((n, MLA_NOPE), MLA_SCALE, F32), cc * MLA_SCALE, z(MLA_ROPE)], axis=1)
    stab = jnp.concatenate([z(MLA_NOPE + MLA_ROPE), ss * MLA_SCALE], axis=1)
    ck = jnp.concatenate([z(MLA_NOPE), cc, z(MLA_ROPE)], axis=1)
    sk = jnp.concatenate([z(MLA_NOPE + MLA_ROPE), ss], axis=1)
    return jnp.concatenate([ctab, stab, ck, sk], axis=1)


def _cmp_map(n_rows, n_sel, n_lanes):
    c0 = np.arange(n_rows)[:, None] * CMP_STRIDE
    s0 = np.arange(n_lanes)[None, :] * SEL_BLOCK
    m = (c0 < s0 + SEL_BLOCK) & (c0 + CMP_BLOCK > s0) & (np.arange(n_lanes)[None, :] < n_sel)
    return jnp.asarray(m.astype(np.float32), BF16)


def _prompt_tables(rel_bias, seq):
    t = np.arange(Q_BLOCK)[:, None]
    j = np.arange(LANE)[None, :]
    n_var = -(-(_THR[-1] + LANE) // LANE)
    buckets = [_bucket_np(d * LANE + t - j) for d in range(n_var)] + [np.full((Q_BLOCK, LANE), REL_BUCKETS - 1)]
    tbsel = _bias_tables(rel_bias, np.stack(buckets))
    jw = np.arange(WINDOW + Q_BLOCK)[None, :]
    dist_w = WINDOW + t - jw
    tbwin = _bias_tables(rel_bias, _masked_bucket(dist_w, (dist_w >= 0) & (dist_w < WINDOW))[None])[0]
    n_cmp = seq // CMP_STRIDE
    cmap = _cmp_map(n_cmp, seq // SEL_BLOCK, LANE)
    far = _THR[-1] + CMP_STRIDE * (LANE - 1) + CMP_BLOCK - 1
    n_varc = -(-far // Q_BLOCK)
    buckets = [_bucket_np(k * Q_BLOCK + t - (j * CMP_STRIDE + CMP_BLOCK - 1)) for k in range(n_varc)]
    buckets.append(np.full((Q_BLOCK, LANE), REL_BUCKETS - 1))
    tbcmp = _bias_tables(rel_bias, np.stack(buckets))
    return tbcmp, tbsel, tbwin, cmap


def _sample_tables(rel_bias, past, t_new, wb):
    t = np.arange(t_new)[:, None]
    n_cmp = past // CMP_STRIDE
    n = np.arange(n_cmp)[None, :]
    dist_c = past + t - (n * CMP_STRIDE + CMP_BLOCK - 1)
    ok_c = (dist_c >= 0) & (n < n_cmp - 1)
    tbc = _bias_tables(rel_bias, _masked_bucket(dist_c, ok_c)[None])[0]
    k = np.arange(past + PAGE_SIZE)[None, :]
    dist_s = past + t - k
    ok_s = (dist_s >= 0) & (k < past + t_new)
    tbs = _bias_tables(rel_bias, _masked_bucket(dist_s, ok_s)[None])[0]
    i = np.arange(wb + PAGE_SIZE)[None, :]
    dist_w = wb + t - i
    ok_w = (dist_w >= 0) & (dist_w < WINDOW) & (i < wb + t_new)
    tbw = _bias_tables(rel_bias, _masked_bucket(dist_w, ok_w)[None])[0]
    n_sel = past // SEL_BLOCK + 1
    n_lanes = -(-n_sel // LANE) * LANE
    place = np.zeros((NSA_HEADS, LANE, NSA_HEADS * HEAD_DIM), np.float32)
    for h in range(NSA_HEADS):
        place[h, HEAD_DIM + np.arange(HEAD_DIM), h * HEAD_DIM + np.arange(HEAD_DIM)] = 1.0
    expand = (np.arange(n_lanes)[:, None] == (np.arange(past + PAGE_SIZE)[None, :] // SEL_BLOCK)).astype(np.float32)
    return dict(tbc_s=tbc, tbs_s=tbs, tbw_s=tbw, map_s=_cmp_map(n_cmp, n_sel, n_lanes),
                place=jnp.asarray(place.reshape(NSA_HEADS * LANE, NSA_HEADS * HEAD_DIM), BF16),
                expand_s=jnp.asarray(expand, BF16))


def kernel(x_prompt, x_sample, cache_nsa, cache_mla, cache_mla_kscale, state_win, state_conv, page_table,
           rel_bias, attn_norm, w_in, nsa_norm, cmp_pe, cmp_w1, cmp_w2, mla_q_lat_norm, mla_kv_norm,
           w_q_up, w_uk, w_uv, mla_nope_norm, mla_rope_norm, w_o, ffn_norm, w_up, conv_w, conv_b, w_down):
    p = dict(attn_norm=attn_norm, w_in=w_in, nsa_norm=nsa_norm, cmp_pe=cmp_pe, cmp_w1=cmp_w1, cmp_w2=cmp_w2,
             mla_q_lat_norm=mla_q_lat_norm, mla_kv_norm=mla_kv_norm, w_q_up=w_q_up, w_uk=w_uk, w_uv=w_uv,
             mla_nope_norm=mla_nope_norm, mla_rope_norm=mla_rope_norm, w_o=w_o, ffn_norm=ffn_norm, w_up=w_up,
             conv_w=conv_w, conv_b=conv_b, w_down=w_down)
    depth = w_in.shape[0]
    nb, seq, _ = x_prompt.shape
    db, t_new, _ = x_sample.shape
    n_pool = cache_nsa.shape[1]
    n_pages = page_table.shape[1]
    past = n_pages * PAGE_SIZE
    wb = state_win.shape[2]
    assert seq % MLA_TQ == 0 and n_pages % PAGES_PER_STEP == 0 and past % (8 * PAGE_SIZE) == 0
    assert wb == WINDOW and t_new == 8 and seq >= WINDOW

    tm_p = 256
    tm_f = 512 if seq % 512 == 0 else 256
    ts = db * t_new
    tm_s = 128 if ts % 128 == 0 else ts
    rope_p = _rope_table(jnp.arange(seq, dtype=jnp.int32))
    rope_s = jnp.tile(_rope_table(past + jnp.arange(t_new, dtype=jnp.int32)), (db, 1))
    tbcmp, tbsel, tbwin, cmap = _prompt_tables(rel_bias, seq)
    tabs = _sample_tables(rel_bias, past, t_new, wb)
    nsa_pages_t = jnp.transpose(cache_nsa, (0, 1, 3, 4, 5, 2)).reshape(depth * n_pool, 4 * HEAD_DIM, PAGE_SIZE)
    mla_pages_t = jnp.transpose(cache_mla, (0, 1, 3, 2)).reshape(depth * n_pool, KV_LORA + MLA_ROPE, PAGE_SIZE)
    ks_pages_t = jnp.transpose(cache_mla_kscale, (0, 1, 3, 2)).reshape(depth * n_pool, MLA_HEADS, PAGE_SIZE)
    state_t = jnp.transpose(state_win, (0, 1, 3, 4, 5, 2)).reshape(depth, db, LANE, wb)

    xp = x_prompt.reshape(nb * seq, D_MODEL)
    xs = x_sample.reshape(ts, D_MODEL)
    outs_p = [[] for _ in range(5)]
    outs_s = [[] for _ in range(5)]
    for l in range(depth):
        lw = _layer_weights(p, l)
        (q_nsa, nsarows, winrows, cmp16, sel16, win16, gates, q_mla, mlarows, kscale, kfull, v16) = _proj(
            xp, lw, rope_p, tm_p, BF16)
        kvc = _compress(cmp16.reshape(nb * seq // CMP_STRIDE, CMP_STRIDE * LANE), lw, nb)
        win16p = jnp.pad(win16.reshape(nb, seq, LANE), ((0, 0), (WINDOW, 0), (0, 0)))
        mix_nsa = _nsa_prompt(q_nsa, kvc, sel16, win16p, gates, tbcmp, tbsel, tbwin, cmap, nb, seq)
        mix_mla = _mla_prompt(q_mla, kfull, v16, nb, seq, MLA_TQ)
        xp, tail = _ffn(xp, mix_nsa, mix_mla, lw, tm_f, tiles_per_seq=seq // tm_f)
        tiles = seq // tm_f
        last = tail.reshape(nb, tiles, 2 * N_FF_CHUNKS, 8, FF_CHUNK)[:, tiles - 1, :, 8 - (CONV_W - 1):, :]
        outs_p[0].append(nsarows.reshape(nb, seq, 4, 1, HEAD_DIM))
        outs_p[1].append(mlarows.reshape(nb, seq, KV_LORA + MLA_ROPE))
        outs_p[2].append(kscale[:, :MLA_HEADS].reshape(nb, seq, MLA_HEADS))
        outs_p[3].append(winrows.reshape(nb, seq, 2, 1, HEAD_DIM)[:, seq - min(WINDOW, seq):])
        outs_p[4].append(jnp.transpose(last, (0, 2, 1, 3)).reshape(nb, CONV_W - 1, 2 * D_FF))
        (q_nsa, nsarows, winrows, _, _, _, gates, q_mla, mlarows, kscale, _, _) = _proj(xs, lw, rope_s, ts, F32)
        pt_l = page_table + l * n_pool
        mix_nsa = _nsa_sample(pt_l, nsa_pages_t, q_nsa, nsarows, winrows, state_t[l], gates, lw, tabs, db, t_new,
                              n_pages)
        ksnew_t = jnp.swapaxes(kscale[:, :MLA_HEADS].reshape(db, t_new, MLA_HEADS), 1, 2)
        mix_mla = _mla_sample(pt_l, mla_pages_t, ks_pages_t, q_mla, mlarows, ksnew_t, lw, db, t_new, n_pages)
        sc = state_conv[l]
        zpad = jnp.zeros((db, t_new - 1, 2 * D_FF), F32)
        a1 = jnp.concatenate([sc[:, 1:2], zpad], axis=1)
        a2 = jnp.concatenate([sc, zpad[:, 1:]], axis=1)
        chunked = lambda a: jnp.transpose(a.reshape(ts, 2 * N_FF_CHUNKS, FF_CHUNK), (1, 0, 2))
        xs, tail = _ffn(xs, mix_nsa.astype(BF16), mix_mla.astype(BF16), lw, tm_s, period=t_new,
                        a1=chunked(a1), a2=chunked(a2))
        h_rows = jnp.transpose(tail, (0, 2, 1, 3)).reshape(db, t_new, 2 * D_FF)
        outs_s[0].append(nsarows.reshape(db, t_new, 4, 1, HEAD_DIM))
        outs_s[1].append(mlarows.reshape(db, t_new, KV_LORA + MLA_ROPE))
        outs_s[2].append(kscale[:, :MLA_HEADS].reshape(db, t_new, MLA_HEADS))
        outs_s[3].append(jnp.concatenate([state_win[l][:, t_new:], winrows.reshape(db, t_new, 2, 1, HEAD_DIM)], axis=1))
        outs_s[4].append(h_rows[:, t_new - (CONV_W - 1):])
    stack = lambda o: jnp.stack(o, axis=0)
    nsa_p, mla_p, ks_p, win_p, cv_p = [stack(o) for o in outs_p]
    nsa_s, mla_s, ks_s, win_s, cv_s = [stack(o) for o in outs_s]
    return (xp.reshape(nb, seq, D_MODEL), xs.reshape(db, t_new, D_MODEL), nsa_p, mla_p, ks_p, win_p, cv_p,
            nsa_s, mla_s, ks_s, win_s, cv_s)
```

```python
import functools
import math

import numpy as np
import jax
import jax.numpy as jnp
from jax import lax
from jax.experimental import pallas as pl
from jax.experimental.pallas import tpu as pltpu

F32 = jnp.float32
BF16 = jnp.bfloat16

D_MODEL = 1024
PAGE_SIZE = 128
HEAD_DIM = 64
NSA_HEADS = 8
CMP_STRIDE = 16
CMP_BLOCK = 32
CMP_HIDDEN = 64
SEL_BLOCK = 64
SEL_TOPK = 16
WINDOW = 512
MLA_HEADS = 8
MLA_NOPE = 64
MLA_ROPE = 32
MLA_V = 64
Q_LORA = 384
KV_LORA = 256
ROPE_THETA = 10000.0
D_FF = 2816
CONV_W = 3
REL_BUCKETS = 32
REL_MAX_DIST = 1024
EPS = 1e-6
Q_BLOCK = 128
NSA_SCALE = HEAD_DIM ** -0.5
MLA_SCALE = (MLA_NOPE + MLA_ROPE) ** -0.5

OFF_NSA_KV = NSA_HEADS * HEAD_DIM
OFF_GATE = OFF_NSA_KV + 6 * HEAD_DIM
OFF_QLAT = OFF_GATE + 3 * NSA_HEADS
OFF_CKV = OFF_QLAT + Q_LORA
OFF_KROPE = OFF_CKV + KV_LORA
IN_COLS = OFF_KROPE + MLA_ROPE

LANE = 128
NEG = -1e30
VMEM_LIMIT = 56 * 1024 * 1024

C_Q = 0
C_KV = C_Q + NSA_HEADS * LANE
C_QL = C_KV + 3 * LANE
C_CKV = C_QL + Q_LORA
C_E = C_CKV + KV_LORA
C_END = C_E + LANE

FF_CHUNK = 256
N_FF_CHUNKS = D_FF // FF_CHUNK
FF_GROUP = 3
PAGES_PER_STEP = 16
SAMPLE_SEQS = 4
SEL_TK = 512
MLA_TQ = 256
MLA_TK = 512


def _dot(a, b):
    return jnp.dot(a, b, preferred_element_type=F32)


def _dot_nt(a, b):
    return lax.dot_general(a, b, (((1,), (1,)), ((), ())), preferred_element_type=F32)


def _params(sem=None):
    return pltpu.CompilerParams(dimension_semantics=sem, vmem_limit_bytes=VMEM_LIMIT)


def _full(shape):
    n = len(shape)
    return pl.BlockSpec(shape, lambda *_: (0,) * n)


def _bucket_thresholds():
    thr = list(range(1, REL_BUCKETS // 2 + 1))
    for k in range(1, REL_BUCKETS // 2):
        d = int(2.0 ** ((3 * k + 32) / 8.0)) - 2
        while d ** 8 < 2 ** (3 * k + 32):
            d += 1
        thr.append(d)
    return thr


_THR = _bucket_thresholds()


def _bucket_np(dist):
    n = np.maximum(np.asarray(dist, np.int64), 0)
    out = np.zeros(n.shape, np.int32)
    for t in _THR:
        out += (n >= t).astype(np.int32)
    return out


def _bias_lookup_kernel(rb_ref, idx_ref, out_ref):
    idx = idx_ref[...]
    vals = [jnp.full(idx.shape, NEG, F32) for _ in range(NSA_HEADS)]
    for k in range(REL_BUCKETS):
        hit = idx == k
        vals = [jnp.where(hit, rb_ref[k, h], vals[h]) for h in range(NSA_HEADS)]
    for h in range(NSA_HEADS):
        out_ref[h] = vals[h]


def _bias_tables(rel_bias, buckets):
    n, t, k = buckets.shape
    out = pl.pallas_call(
        _bias_lookup_kernel,
        grid=(n,),
        in_specs=[pl.BlockSpec(memory_space=pltpu.SMEM), pl.BlockSpec((None, t, k), lambda i: (i, 0, 0))],
        out_specs=pl.BlockSpec((None, NSA_HEADS, t, k), lambda i: (i, 0, 0, 0)),
        out_shape=jax.ShapeDtypeStruct((n, NSA_HEADS, t, k), F32),
        compiler_params=_params(("parallel",)),
        name="bias_lookup",
    )(rel_bias, jnp.asarray(buckets, jnp.int32))
    return out.reshape(n, NSA_HEADS * t, k)


def _masked_bucket(dist, ok):
    return np.where(ok, _bucket_np(dist), REL_BUCKETS).astype(np.int32)


def _proj_kernel(x_ref, gattn_ref, w1_ref, gq_ref, gkv_ref, gql_ref, wqup_ref, gmq_ref, gc_ref,
                 wuk_ref, wuv_ref, ge_ref, rope_ref,
                 qnsa_ref, nsarows_ref, winrows_ref, cmp16_ref, sel16_ref, win16_ref, gates_ref,
                 qmla_ref, mlarows_ref, kscale_ref, kfull_ref, v16_ref):
    tm = x_ref.shape[0]
    lane = lax.broadcasted_iota(jnp.int32, (tm, LANE), 1)
    lo = lane < HEAD_DIM
    x = x_ref[...]
    xn = x * lax.rsqrt(jnp.mean(x * x, axis=-1, keepdims=True) + EPS) * gattn_ref[...]
    xn16 = xn.astype(BF16)

    qb = _dot(xn16, w1_ref[:, C_Q:C_KV])
    for h in range(NSA_HEADS):
        b = qb[:, h * LANE:(h + 1) * LANE]
        rs = lax.rsqrt(jnp.sum(b * b, axis=-1, keepdims=True) * (1.0 / HEAD_DIM) + EPS)
        qnsa_ref[:, h * LANE:(h + 1) * LANE] = (b * rs * gq_ref[...]).astype(qnsa_ref.dtype)

    kv = _dot(xn16, w1_ref[:, C_KV:C_QL])
    b0 = kv[:, 0:LANE]
    nsarows_ref[:, 0:LANE] = b0
    cmp16_ref[...] = b0.astype(BF16)
    for j, (dst32, dst16) in enumerate(((nsarows_ref, sel16_ref), (winrows_ref, win16_ref))):
        b = kv[:, (j + 1) * LANE:(j + 2) * LANE]
        ss = jnp.sum(jnp.where(lo, b * b, 0.0), axis=-1, keepdims=True)
        rs = lax.rsqrt(ss * (1.0 / HEAD_DIM) + EPS)
        bn = b * jnp.where(lo, rs, 1.0) * gkv_ref[j:j + 1, :]
        if j == 0:
            dst32[:, LANE:2 * LANE] = bn
        else:
            dst32[...] = bn
        dst16[...] = bn.astype(BF16)

    e = _dot(xn16, w1_ref[:, C_E:C_END])
    gates_ref[...] = 1.0 / (1.0 + jnp.exp(-e))
    rope_lanes = (lane >= HEAD_DIM) & (lane < HEAD_DIM + MLA_ROPE)
    sse = jnp.sum(jnp.where(rope_lanes, e * e, 0.0), axis=-1, keepdims=True)
    en = e * lax.rsqrt(sse * (1.0 / MLA_ROPE) + EPS) * ge_ref[...]
    krblk = en * rope_ref[:, 2 * LANE:3 * LANE] + pltpu.roll(en * rope_ref[:, 3 * LANE:4 * LANE], LANE - MLA_ROPE, 1)
    mlarows_ref[:, KV_LORA:KV_LORA + MLA_ROPE] = pltpu.roll(krblk, HEAD_DIM, 1)[:, 0:MLA_ROPE]

    ql = _dot(xn16, w1_ref[:, C_QL:C_CKV])
    qln = ql * lax.rsqrt(jnp.mean(ql * ql, axis=-1, keepdims=True) + EPS) * gql_ref[...]
    qa = _dot(qln.astype(BF16), wqup_ref[...])
    ctab = rope_ref[:, 0:LANE]
    stab = rope_ref[:, LANE:2 * LANE]
    for h in range(MLA_HEADS):
        b = qa[:, h * LANE:(h + 1) * LANE]
        b2 = b * b
        ssn = jnp.sum(jnp.where(lo, b2, 0.0), axis=-1, keepdims=True)
        ssr = jnp.sum(jnp.where(rope_lanes, b2, 0.0), axis=-1, keepdims=True)
        scale = jnp.where(lo, lax.rsqrt(ssn * (1.0 / MLA_NOPE) + EPS), lax.rsqrt(ssr * (1.0 / MLA_ROPE) + EPS))
        bn = b * scale * gmq_ref[...]
        q = bn * ctab + pltpu.roll(bn * stab, LANE - MLA_ROPE, 1)
        qmla_ref[:, h * LANE:(h + 1) * LANE] = q.astype(qmla_ref.dtype)

    cb = _dot(xn16, w1_ref[:, C_CKV:C_E])
    c = cb * lax.rsqrt(jnp.mean(cb * cb, axis=-1, keepdims=True) + EPS) * gc_ref[...]
    mlarows_ref[:, 0:KV_LORA] = c
    c16 = c.astype(BF16)
    kn = _dot(c16, wuk_ref[...])
    ksacc = jnp.zeros((tm, LANE), F32)
    for h in range(MLA_HEADS):
        b = kn[:, h * LANE:(h + 1) * LANE]
        ksh = lax.rsqrt(jnp.sum(b * b, axis=-1, keepdims=True) * (1.0 / MLA_NOPE) + EPS)
        ksacc = jnp.where(lane == h, ksh, ksacc)
        kfull_ref[:, h * LANE:(h + 1) * LANE] = (b * ksh + krblk).astype(BF16)
    kscale_ref[...] = ksacc
    v16_ref[...] = _dot(c16, wuv_ref[...]).astype(BF16)


def _proj(x, lw, rope_tab, tm, qdtype):
    t = x.shape[0]
    n_rope = rope_tab.shape[0] // tm
    row = lambda w: pl.BlockSpec((tm, w), lambda i: (i, 0))
    sds = lambda w, dt: jax.ShapeDtypeStruct((t, w), dt)
    in_specs = [row(D_MODEL), _full((1, D_MODEL)), _full((D_MODEL, C_END)), _full((1, LANE)), _full((8, LANE)),
                _full((1, Q_LORA)), _full((Q_LORA, MLA_HEADS * LANE)), _full((1, LANE)), _full((1, KV_LORA)),
                _full((KV_LORA, MLA_HEADS * LANE)), _full((KV_LORA, MLA_HEADS * MLA_V)), _full((1, LANE)),
                pl.BlockSpec((tm, 4 * LANE), lambda i: (i % n_rope, 0))]
    widths = [(NSA_HEADS * LANE, qdtype), (2 * LANE, F32), (LANE, F32), (LANE, BF16), (LANE, BF16), (LANE, BF16),
              (LANE, F32), (MLA_HEADS * LANE, qdtype), (KV_LORA + MLA_ROPE, F32), (LANE, F32),
              (MLA_HEADS * LANE, BF16), (MLA_HEADS * MLA_V, BF16)]
    return pl.pallas_call(
        _proj_kernel,
        grid=(t // tm,),
        in_specs=in_specs,
        out_specs=[row(w) for w, _ in widths],
        out_shape=[sds(w, dt) for w, dt in widths],
        compiler_params=_params(("parallel",)),
        name="proj",
    )(x, lw["gattn"], lw["w1"], lw["gq"], lw["gkv"], lw["gql"], lw["wqup"], lw["gmq"], lw["gc"],
      lw["wuk"], lw["wuv"], lw["ge"], rope_tab)


def _compress_tail(h, pe_r, w2, gk, lane_lo):
    n = h.shape[0]
    pe = pe_r[0:1, 0:LANE] + pe_r[1:2, LANE:2 * LANE]
    hid = jax.nn.gelu(h[:, 0:LANE] + pltpu.roll(h[:, LANE:2 * LANE], n - 1, 0) + pe, approximate=True)
    out = _dot(hid.astype(BF16), w2)
    ss = jnp.sum(jnp.where(lane_lo, out * out, 0.0), axis=-1, keepdims=True)
    return out * jnp.where(lane_lo, lax.rsqrt(ss * (1.0 / HEAD_DIM) + EPS), 1.0) * gk


def _compress_kernel(x_ref, w1_ref, pe_ref, w2_ref, gk_ref, out_ref):
    n = x_ref.shape[0]
    lane_lo = lax.broadcasted_iota(jnp.int32, (n, LANE), 1) < HEAD_DIM
    h = _dot(x_ref[...], w1_ref[...])
    pe_r = _dot(pe_ref[...], w1_ref[...])
    out_ref[...] = _compress_tail(h, pe_r, w2_ref[...], gk_ref[...], lane_lo).astype(BF16)


def _compress(chunks, lw, n_batch):
    n = chunks.shape[0] // n_batch
    kw = CMP_STRIDE * LANE
    return pl.pallas_call(
        _compress_kernel,
        grid=(n_batch,),
        in_specs=[pl.BlockSpec((n, kw), lambda b: (b, 0)), _full((kw, 2 * LANE)), _full((8, kw)),
                  _full((LANE, LANE)), _full((1, LANE))],
        out_specs=pl.BlockSpec((n, LANE), lambda b: (b, 0)),
        out_shape=jax.ShapeDtypeStruct((chunks.shape[0], LANE), BF16),
        compiler_params=_params(("parallel",)),
        name="compress",
    )(chunks, lw["cw1"], lw["cpe"], lw["cw2"], lw["gkc"])


def _topk_mask(score, lane):
    sel = jnp.zeros(score.shape, F32)
    cur = score
    for _ in range(SEL_TOPK):
        pick = lane == jnp.argmax(cur, axis=-1, keepdims=True)
        sel = jnp.where(pick, 1.0, sel)
        cur = jnp.where(pick, -jnp.inf, cur)
    return sel


def _softmax_rows(s, valid):
    m = jnp.max(s, axis=-1, keepdims=True)
    e = jnp.where(valid, jnp.exp(s - m), 0.0)
    return e / jnp.maximum(jnp.sum(e, axis=-1, keepdims=True), 1e-30)


def _nsa_prompt_kernel(q_ref, kvc_ref, sel_ref, win_ref, gates_ref, tbcmp_ref, tbsel_ref, tbwin_ref, map_ref,
                       mix_ref, m_ref, acc_ref):
    qi = pl.program_id(1)
    q0 = qi * Q_BLOCK
    n_cmp = kvc_ref.shape[0]
    qblk = q_ref[...]
    qh = [qblk[:, h * LANE:(h + 1) * LANE] for h in range(NSA_HEADS)]

    kvc = kvc_ref[...]
    cw = min(LANE, n_cmp)
    t_i = lax.broadcasted_iota(jnp.int32, (Q_BLOCK, n_cmp), 0)
    n_i = lax.broadcasted_iota(jnp.int32, (Q_BLOCK, n_cmp), 1)
    valid_c = q0 + t_i - (n_i * CMP_STRIDE + (CMP_BLOCK - 1)) >= 0
    ridx = [jnp.clip(qi - (LANE * CMP_STRIDE // Q_BLOCK) * blk, 0, tbcmp_ref.shape[0] - 1)
            for blk in range(n_cmp // cw)]
    cext = jnp.concatenate([kvc, map_ref[...], jnp.ones((n_cmp, LANE), BF16)], axis=1)
    o_c, imp = [], None
    for h in range(NSA_HEADS):
        r = slice(h * Q_BLOCK, (h + 1) * Q_BLOCK)
        tb = jnp.concatenate([tbcmp_ref[i, r, 0:cw] for i in ridx], axis=1)
        s = jnp.where(valid_c, _dot_nt(qh[h], kvc) + tb, NEG)
        e = jnp.where(valid_c, jnp.exp(s - jnp.max(s, axis=-1, keepdims=True)), 0.0)
        res = _dot(e.astype(BF16), cext)
        inv = 1.0 / jnp.maximum(res[:, 2 * LANE:3 * LANE], 1e-30)
        o_c.append(res[:, 0:LANE] * inv)
        imp = res[:, LANE:2 * LANE] * inv if imp is None else imp + res[:, LANE:2 * LANE] * inv

    n_sel = sel_ref.shape[0] // SEL_BLOCK
    lane = lax.broadcasted_iota(jnp.int32, (Q_BLOCK, LANE), 1)
    qpos = q0 + lax.broadcasted_iota(jnp.int32, (Q_BLOCK, LANE), 0)
    q_blk = qpos // SEL_BLOCK
    forced = (lane == 0) | (lane == q_blk) | (lane == q_blk - 1)
    valid_b = lane * SEL_BLOCK <= qpos
    score = jnp.where(valid_b & (lane < n_sel), jnp.where(forced, jnp.inf, imp), -jnp.inf)
    sel16 = _topk_mask(score, lane).astype(BF16)

    n_w = WINDOW + Q_BLOCK
    kvw = win_ref[pl.ds(pl.multiple_of(q0, Q_BLOCK), n_w), :]
    col_w = lax.broadcasted_iota(jnp.int32, (Q_BLOCK, n_w), 1)
    pos_mask = jnp.where(q0 - WINDOW + col_w >= 0, 0.0, NEG)
    wext = jnp.concatenate([kvw, jnp.ones((n_w, LANE), BF16)], axis=1)
    o_w = []
    for h in range(NSA_HEADS):
        s = _dot_nt(qh[h], kvw) + (tbwin_ref[h * Q_BLOCK:(h + 1) * Q_BLOCK, :] + pos_mask)
        e = jnp.exp(s - jnp.max(s, axis=-1, keepdims=True))
        res = _dot(e.astype(BF16), wext)
        o_w.append(res[:, 0:LANE] / res[:, LANE:2 * LANE])

    m_ref[...] = jnp.full(m_ref.shape, NEG, F32)
    acc_ref[...] = jnp.zeros(acc_ref.shape, F32)
    blk_i = lax.broadcasted_iota(jnp.int32, (LANE, SEL_TK), 0)
    col_i = lax.broadcasted_iota(jnp.int32, (LANE, SEL_TK), 1)
    row_t = lax.broadcasted_iota(jnp.int32, (Q_BLOCK, SEL_TK), 0)
    col_t = lax.broadcasted_iota(jnp.int32, (Q_BLOCK, SEL_TK), 1)
    n_const = tbsel_ref.shape[0] - 1
    ones = jnp.ones((SEL_TK, LANE), BF16)

    def sel_step(kt, carry):
        k0 = pl.multiple_of(kt * SEL_TK, SEL_TK)
        kv = sel_ref[pl.ds(k0, SEL_TK), :]
        kvext = jnp.concatenate([kv, ones], axis=1)
        expand = jnp.where(blk_i == (k0 + col_i) // SEL_BLOCK, 1.0, 0.0).astype(BF16)
        mk = _dot(sel16, expand)
        add_mask = jnp.where((mk > 0.5) & (k0 + col_t <= q0 + row_t), 0.0, NEG)
        didx = [jnp.clip((q0 - k0) // LANE - j, 0, n_const) for j in range(SEL_TK // LANE)]
        ss = [_dot_nt(qh[h], kv) for h in range(NSA_HEADS)]
        ps, alphas = [], []
        for h in range(NSA_HEADS):
            r = slice(h * Q_BLOCK, (h + 1) * Q_BLOCK)
            tb = jnp.concatenate([tbsel_ref[d, r, :] for d in didx], axis=1)
            s = ss[h] + (tb + add_mask)
            m_old = m_ref[h]
            m_new = jnp.maximum(m_old, jnp.max(s, axis=-1, keepdims=True))
            alphas.append(jnp.exp(m_old - m_new))
            ps.append(jnp.exp(s - jnp.concatenate([m_new] * (SEL_TK // LANE), axis=1)).astype(BF16))
            m_ref[h] = m_new
        for h in range(NSA_HEADS):
            acc_ref[h] = jnp.concatenate([alphas[h]] * 2, axis=1) * acc_ref[h] + _dot(ps[h], kvext)
        return carry

    lax.fori_loop(0, q0 // SEL_TK + 1, sel_step, 0)

    g = gates_ref[...]
    mixed = []
    for h in range(NSA_HEADS):
        r = slice(h * Q_BLOCK, (h + 1) * Q_BLOCK)
        o_s = acc_ref[h, :, 0:LANE] / acc_ref[h, :, LANE:2 * LANE]
        mixed.append(g[:, h:h + 1] * o_c[h] + g[:, NSA_HEADS + h:NSA_HEADS + h + 1] * o_s
                     + g[:, 2 * NSA_HEADS + h:2 * NSA_HEADS + h + 1] * o_w[h])
    for j in range(NSA_HEADS // 2):
        slab = jnp.where(lane < HEAD_DIM, pltpu.roll(mixed[2 * j], HEAD_DIM, 1), mixed[2 * j + 1])
        mix_ref[:, j * LANE:(j + 1) * LANE] = slab.astype(BF16)


def _nsa_prompt(q, kvc, sel16, win16p, gates, tbcmp, tbsel, tbwin, cmap, n_batch, seq):
    nqb = seq // Q_BLOCK
    n_cmp = seq // CMP_STRIDE
    const = lambda a: pl.BlockSpec(a.shape, lambda b, i: (0,) * a.ndim, pipeline_mode=pl.Buffered(1))
    return pl.pallas_call(
        _nsa_prompt_kernel,
        grid=(n_batch, nqb),
        in_specs=[pl.BlockSpec((Q_BLOCK, NSA_HEADS * LANE), lambda b, i: (b * nqb + i, 0)),
                  pl.BlockSpec((n_cmp, LANE), lambda b, i: (b, 0)),
                  pl.BlockSpec((seq, LANE), lambda b, i: (b, 0)),
                  pl.BlockSpec((None, seq + WINDOW, LANE), lambda b, i: (b, 0, 0)),
                  pl.BlockSpec((Q_BLOCK, LANE), lambda b, i: (b * nqb + i, 0)),
                  const(tbcmp), const(tbsel), const(tbwin), const(cmap)],
        out_specs=pl.BlockSpec((Q_BLOCK, NSA_HEADS * HEAD_DIM), lambda b, i: (b * nqb + i, 0)),
        scratch_shapes=[pltpu.VMEM((NSA_HEADS, Q_BLOCK, LANE), F32), pltpu.VMEM((NSA_HEADS, Q_BLOCK, 2 * LANE), F32)],
        out_shape=jax.ShapeDtypeStruct((n_batch * seq, NSA_HEADS * HEAD_DIM), BF16),
        compiler_params=_params(("parallel", "parallel")),
        name="nsa_prompt",
    )(q, kvc, sel16, win16p, gates, tbcmp, tbsel, tbwin, cmap)


def _mla_prompt_kernel(q_ref, k_ref, v_ref, o_ref, m_ref, acc_ref):
    qi = pl.program_id(1)
    tq = q_ref.shape[0]
    tk = MLA_TK
    q0 = qi * tq
    lane = lax.broadcasted_iota(jnp.int32, (tq, LANE), 1)
    m_ref[...] = jnp.full(m_ref.shape, NEG, F32)
    acc_ref[...] = jnp.zeros(acc_ref.shape, F32)
    ones = jnp.ones((tk, LANE), BF16)

    def step(kt, mask):
        k0 = pl.multiple_of(kt * tk, tk)
        ss = [_dot_nt(q_ref[:, h * LANE:(h + 1) * LANE], k_ref[pl.ds(k0, tk), h * LANE:(h + 1) * LANE])
              for h in range(MLA_HEADS)]
        ps, alphas = [], []
        for h in range(MLA_HEADS):
            s = ss[h] if mask is None else ss[h] + mask
            m_old = m_ref[h]
            m_new = jnp.maximum(m_old, jnp.max(s, axis=-1, keepdims=True))
            alphas.append(jnp.exp(m_old - m_new))
            ps.append(jnp.exp(s - jnp.concatenate([m_new] * (tk // LANE), axis=1)).astype(BF16))
            m_ref[h] = m_new
        for h in range(MLA_HEADS):
            vl = (h // 2) * LANE
            vext = jnp.concatenate([v_ref[pl.ds(k0, tk), vl:vl + LANE], ones], axis=1)
            acc_ref[h] = jnp.concatenate([alphas[h]] * 2, axis=1) * acc_ref[h] + _dot(ps[h], vext)

    def body(kt, carry):
        step(kt, None)
        return carry

    n_full = q0 // tk
    lax.fori_loop(0, n_full, body, 0)
    row_i = lax.broadcasted_iota(jnp.int32, (tq, tk), 0)
    col_i = lax.broadcasted_iota(jnp.int32, (tq, tk), 1)
    step(n_full, jnp.where(n_full * tk + col_i <= q0 + row_i, 0.0, NEG))
    for j in range(MLA_HEADS // 2):
        lo = acc_ref[2 * j, :, 0:LANE] / acc_ref[2 * j, :, LANE:2 * LANE]
        hi = acc_ref[2 * j + 1, :, 0:LANE] / acc_ref[2 * j + 1, :, LANE:2 * LANE]
        o_ref[:, j * LANE:(j + 1) * LANE] = jnp.where(lane < MLA_V, lo, hi).astype(BF16)


def _mla_prompt(q, kfull, v16, n_batch, seq, tq):
    nq = seq // tq
    return pl.pallas_call(
        _mla_prompt_kernel,
        grid=(n_batch, nq),
        in_specs=[pl.BlockSpec((tq, MLA_HEADS * LANE), lambda b, i: (b * nq + i, 0)),
                  pl.BlockSpec((seq, MLA_HEADS * LANE), lambda b, i: (b, 0)),
                  pl.BlockSpec((seq, MLA_HEADS * MLA_V), lambda b, i: (b, 0))],
        out_specs=pl.BlockSpec((tq, MLA_HEADS * MLA_V), lambda b, i: (b * nq + i, 0)),
        out_shape=jax.ShapeDtypeStruct((n_batch * seq, MLA_HEADS * MLA_V), BF16),
        scratch_shapes=[pltpu.VMEM((MLA_HEADS, tq, LANE), F32), pltpu.VMEM((MLA_HEADS, tq, 2 * LANE), F32)],
        compiler_params=_params(("parallel", "parallel")),
        name="mla_prompt",
    )(q, kfull, v16)


def _ffn_kernel(x_ref, mixa_ref, mixb_ref, wo_ref, gffn_ref, wup_ref, cw_ref, wdn_ref, *rest, period, tiles_per_seq):
    if period is None:
        y_ref, tail_ref, hn_ref, h_ref, acc_ref, bufu_ref, bufg_ref, carry_ref = rest
    else:
        a1_ref, a2_ref, y_ref, tail_ref, hn_ref, h_ref, acc_ref, bufu_ref, bufg_ref = rest
    tm = x_ref.shape[0]
    half = wo_ref.shape[0] // 2
    h = x_ref[...] + _dot(mixa_ref[...], wo_ref[0:half, :]) + _dot(mixb_ref[...], wo_ref[half:, :])
    h_ref[...] = h
    hn = h * lax.rsqrt(jnp.mean(h * h, axis=-1, keepdims=True) + EPS) * gffn_ref[...]
    hn_ref[...] = hn.astype(BF16)
    acc_ref[...] = jnp.zeros(acc_ref.shape, F32)
    if period is None:
        first = pl.program_id(0) % tiles_per_seq == 0
    else:
        tmod = lax.broadcasted_iota(jnp.int32, (tm, FF_CHUNK), 0) % period

    def chunk(c, slot, outs):
        hn16 = hn_ref[...]
        convd = []
        for part, buf in ((0, bufu_ref), (1, bufg_ref)):
            idx = part * N_FF_CHUNKS + c
            hc = _dot(hn16, wup_ref[idx])
            yield
            buf[slot, 8:8 + tm, :] = hc
            tail_ref[idx] = hc[tm - tail_ref.shape[1]:tm, :]
            if period is None:
                prev = carry_ref[idx]
                buf[slot, 0:8, :] = jnp.where(first, 0.0, prev)
                carry_ref[idx] = hc[tm - 8:tm, :]
                h1 = buf[slot, 7:7 + tm, :]
                h2 = buf[slot, 6:6 + tm, :]
            else:
                buf[slot, 0:8, :] = jnp.zeros((8, FF_CHUNK), F32)
                h1 = jnp.where(tmod >= 1, buf[slot, 7:7 + tm, :], a1_ref[idx])
                h2 = jnp.where(tmod >= 2, buf[slot, 6:6 + tm, :], a2_ref[idx])
            cw = cw_ref[idx]
            convd.append(cw[3:4, :] + ((h2 * cw[0:1, :] + h1 * cw[1:2, :]) + hc * cw[2:3, :]))
            yield
        u, g = convd
        act = (g * (1.0 / (1.0 + jnp.exp(-g)))) * u
        outs.append(_dot(act.astype(BF16), wdn_ref[c]))

    def run_group(cs):
        outs = []
        _round_robin([chunk(c, slot, outs) for slot, c in enumerate(cs)])
        acc_ref[...] += _tree_sum(outs)

    def group(i, carry):
        run_group([FF_GROUP * i + k for k in range(FF_GROUP)])
        return carry

    n_full = N_FF_CHUNKS // FF_GROUP
    lax.fori_loop(0, n_full, group, 0)
    if N_FF_CHUNKS % FF_GROUP:
        run_group(list(range(n_full * FF_GROUP, N_FF_CHUNKS)))
    y_ref[...] = h_ref[...] + acc_ref[...]


def _ffn(x, mixa, mixb, lw, tm, tiles_per_seq=None, period=None, a1=None, a2=None):
    t = x.shape[0]
    n_tiles = t // tm
    row = lambda w: pl.BlockSpec((tm, w), lambda i: (i, 0))
    once = pl.Buffered(1)
    const = lambda shape: pl.BlockSpec(shape, lambda i: (0,) * len(shape), pipeline_mode=once)
    in_specs = [row(D_MODEL), row(NSA_HEADS * HEAD_DIM), row(MLA_HEADS * MLA_V),
                const((D_MODEL, D_MODEL)), const((1, D_MODEL)),
                const((2 * N_FF_CHUNKS, D_MODEL, FF_CHUNK)), const((2 * N_FF_CHUNKS, 8, FF_CHUNK)),
                const((N_FF_CHUNKS, FF_CHUNK, D_MODEL))]
    args = [x, mixa, mixb, lw["wo"], lw["gffn"], lw["wup"], lw["cw"], lw["wdn"]]
    tail_rows = 8 if period is None else tm
    scratch = [pltpu.VMEM((tm, D_MODEL), BF16), pltpu.VMEM((tm, D_MODEL), F32), pltpu.VMEM((tm, D_MODEL), F32),
               pltpu.VMEM((FF_GROUP, tm + 8, FF_CHUNK), F32), pltpu.VMEM((FF_GROUP, tm + 8, FF_CHUNK), F32)]
    if period is None:
        scratch.append(pltpu.VMEM((2 * N_FF_CHUNKS, 8, FF_CHUNK), F32))
    else:
        in_specs += [pl.BlockSpec((2 * N_FF_CHUNKS, tm, FF_CHUNK), lambda i: (0, i, 0))] * 2
        args += [a1, a2]
    return pl.pallas_call(
        functools.partial(_ffn_kernel, period=period, tiles_per_seq=tiles_per_seq),
        grid=(n_tiles,),
        in_specs=in_specs,
        out_specs=[row(D_MODEL), pl.BlockSpec((None, 2 * N_FF_CHUNKS, tail_rows, FF_CHUNK), lambda i: (i, 0, 0, 0))],
        out_shape=[jax.ShapeDtypeStruct((t, D_MODEL), F32),
                   jax.ShapeDtypeStruct((n_tiles, 2 * N_FF_CHUNKS, tail_rows, FF_CHUNK), F32)],
        scratch_shapes=scratch,
        compiler_params=_params(("arbitrary",)),
        name="ffn",
    )(*args)


def _tree_sum(xs):
    while len(xs) > 1:
        xs = [xs[i] + xs[i + 1] for i in range(0, len(xs) - 1, 2)] + ([xs[-1]] if len(xs) % 2 else [])
    return xs[0]


def _round_robin(gens):
    live = list(gens)
    while live:
        nxt = []
        for gen in live:
            try:
                next(gen)
                nxt.append(gen)
            except StopIteration:
                pass
        live = nxt


def _nsa_sample_kernel(pt_ref, *refs, n_seq):
    del pt_ref
    pg = PAGES_PER_STEP
    pages = refs[:n_seq * pg]
    (q_ref, rows_ref, winnew_ref, state_ref, gates_ref, w1_ref, pe_ref, w2_ref, gk_ref, map_ref, tbc_ref, tbs_ref,
     tbw_ref, place_ref, expand_ref, mix_ref, hs_ref, sel_ref, stage_ref, qs_ref, s_ref) = refs[n_seq * pg:]
    j = pl.program_id(1)
    n_groups = pl.num_programs(1)
    n_pages = sel_ref.shape[1] - 1
    past = n_pages * PAGE_SIZE
    t_new = q_ref.shape[0] // n_seq
    chunks_per_page = PAGE_SIZE // CMP_STRIDE
    step_keys = pg * PAGE_SIZE

    @pl.when(j == 0)
    def _():
        for g in range(n_seq):
            qblk = q_ref[g * t_new:(g + 1) * t_new, :]
            qs_ref[g] = jnp.concatenate([qblk[:, h * LANE:(h + 1) * LANE] for h in range(NSA_HEADS)],
                                        axis=0).astype(BF16)

    hbase = pl.multiple_of(j * (pg * chunks_per_page), pg * chunks_per_page)
    kbase = pl.multiple_of(j * step_keys, step_keys)
    for g in range(n_seq):
        xs = []
        for k in range(pg):
            page = pages[g * pg + k]
            stage_ref[g * pg + k] = jnp.transpose(page[0:LANE, :])
            xs.append(jnp.concatenate([stage_ref[g * pg + k, pl.ds(r, chunks_per_page, stride=CMP_STRIDE), :]
                                       for r in range(CMP_STRIDE)], axis=1))
            sel_ref[g, j * pg + k] = page[LANE:2 * LANE, :].astype(BF16)
        x = jnp.concatenate(xs, axis=0).astype(BF16)
        hs_ref[g, pl.ds(hbase, pg * chunks_per_page), :] = _dot(x, w1_ref[...])
        kvt_step = jnp.concatenate([sel_ref[g, j * pg + k] for k in range(pg)], axis=1)
        s_ref[g, :, pl.ds(kbase, step_keys)] = _dot(qs_ref[g], kvt_step) + tbs_ref[:, pl.ds(kbase, step_keys)]

    def tail(g):
        tok = slice(g * t_new, (g + 1) * t_new)
        n_cmp = hs_ref.shape[1]
        lane_c = lax.broadcasted_iota(jnp.int32, (n_cmp, LANE), 1) < HEAD_DIM
        pe_r = _dot(pe_ref[...], w1_ref[...])
        kvc = _compress_tail(hs_ref[g], pe_r, w2_ref[...], gk_ref[...], lane_c).astype(BF16)
        qs = qs_ref[g]
        yield

        tbc = tbc_ref[...]
        valid_c = tbc > 0.5 * NEG
        sc = jnp.where(valid_c, _dot_nt(qs, kvc) + tbc, NEG)
        ec = jnp.where(valid_c, jnp.exp(sc - jnp.max(sc, axis=-1, keepdims=True)), 0.0).astype(BF16)
        yield
        n_lane = map_ref.shape[1]
        resc = _dot(ec, jnp.concatenate([kvc, map_ref[...], jnp.ones((n_cmp, LANE), BF16)], axis=1))
        invc = 1.0 / jnp.maximum(resc[:, LANE + n_lane:], 1e-30)
        o_c = resc[:, 0:LANE] * invc
        imp_h = resc[:, LANE:LANE + n_lane] * jnp.concatenate([invc] * (n_lane // LANE), axis=1)
        imp = _tree_sum([imp_h[h * t_new:(h + 1) * t_new, :] for h in range(NSA_HEADS)])

        n_sel = past // SEL_BLOCK + 1
        lane = lax.broadcasted_iota(jnp.int32, (t_new, n_lane), 1)
        qpos = past + lax.broadcasted_iota(jnp.int32, (t_new, n_lane), 0)
        q_blk = qpos // SEL_BLOCK
        forced = (lane == 0) | (lane == q_blk) | (lane == q_blk - 1)
        valid_b = lane * SEL_BLOCK <= qpos
        score = jnp.where(valid_b & (lane < n_sel), jnp.where(forced, jnp.inf, imp), -jnp.inf)
        sel = jnp.zeros(score.shape, F32)
        for _ in range(SEL_TOPK):
            yield
            pick = lane == jnp.argmax(score, axis=-1, keepdims=True)
            sel = jnp.where(pick, 1.0, sel)
            score = jnp.where(pick, -jnp.inf, score)
        sel16 = jnp.concatenate([sel] * NSA_HEADS, axis=0).astype(BF16)

        pad = jnp.zeros((PAGE_SIZE - t_new, LANE), F32)
        newt = jnp.transpose(jnp.concatenate([rows_ref[tok, LANE:2 * LANE], pad], axis=0)).astype(BF16)
        sel_ref[g, n_pages] = newt
        s_ref[g, :, past:past + PAGE_SIZE] = _dot(qs, newt) + tbs_ref[:, past:past + PAGE_SIZE]
        ck = 8 * PAGE_SIZE
        chunks = [(c * ck, ck) for c in range(past // ck)] + [(past, PAGE_SIZE)]
        parts = []
        for k0, w in chunks:
            yield
            hit = _dot(sel16, expand_ref[:, k0:k0 + w])
            s = s_ref[g, :, k0:k0 + w] + jnp.where(hit > 0.5, 0.0, NEG)
            s_ref[g, :, k0:k0 + w] = s
            parts.append(functools.reduce(jnp.maximum, [s[:, i * LANE:(i + 1) * LANE] for i in range(w // LANE)]))
        m_s = jnp.max(functools.reduce(jnp.maximum, parts), axis=-1, keepdims=True)
        accs = []
        for k0, w in chunks:
            yield
            p16 = jnp.exp(s_ref[g, :, k0:k0 + w] - m_s).astype(BF16)
            kvt = jnp.concatenate([sel_ref[g, k0 // PAGE_SIZE + i] for i in range(w // PAGE_SIZE)], axis=1)
            accs.append(_dot_nt(p16, jnp.concatenate([kvt, jnp.ones((LANE, w), BF16)], axis=0)))
        acc = _tree_sum(accs)
        o_s = acc[:, 0:LANE] / acc[:, LANE:2 * LANE]
        yield

        newwt = jnp.transpose(jnp.concatenate([winnew_ref[tok, :], pad], axis=0))
        kvwt = jnp.concatenate([state_ref[g], newwt], axis=1).astype(BF16)
        sw = _dot(qs, kvwt) + tbw_ref[...]
        ew = jnp.exp(sw - jnp.max(sw, axis=-1, keepdims=True)).astype(BF16)
        yield
        rw = _dot_nt(ew, jnp.concatenate([kvwt, jnp.ones((LANE, kvwt.shape[1]), BF16)], axis=0))
        o_w = rw[:, 0:LANE] / rw[:, LANE:2 * LANE]
        yield

        gt = gates_ref[tok, :]
        mixed = []
        for h in range(NSA_HEADS):
            r = slice(h * t_new, (h + 1) * t_new)
            mixed.append((gt[:, h:h + 1] * o_c[r] + gt[:, NSA_HEADS + h:NSA_HEADS + h + 1] * o_s[r]
                          + gt[:, 2 * NSA_HEADS + h:2 * NSA_HEADS + h + 1] * o_w[r]).astype(BF16))
        mix_ref[tok, :] = _dot(jnp.concatenate(mixed, axis=1), place_ref[...])

    @pl.when(j == n_groups - 1)
    def _():
        _round_robin([tail(g) for g in range(n_seq)])


def _nsa_sample(pt_l, pages_t, q, nsarows, winrows, state_t, gates, lw, tabs, n_batch, t_new, n_pages):
    pg = PAGES_PER_STEP
    ns = SAMPLE_SEQS if n_batch % SAMPLE_SEQS == 0 else 1
    n_groups = n_pages // pg
    rows = NSA_HEADS * t_new
    n_cmp = n_pages * PAGE_SIZE // CMP_STRIDE

    def page_spec(g, k):
        return pl.BlockSpec((None, 4 * HEAD_DIM, PAGE_SIZE), lambda b, j, pt: (pt[b * ns + g, j * pg + k], 0, 0))

    per_b = lambda w: pl.BlockSpec((ns * t_new, w), lambda b, j, pt: (b, 0))
    const = lambda a: pl.BlockSpec(a.shape, lambda b, j, pt: (0,) * a.ndim, pipeline_mode=pl.Buffered(1))
    consts = [lw["cw1"], lw["cpe"], lw["cw2"], lw["gkc"], tabs["map_s"], tabs["tbc_s"], tabs["tbs_s"], tabs["tbw_s"],
              tabs["place"], tabs["expand_s"]]
    grid_spec = pltpu.PrefetchScalarGridSpec(
        num_scalar_prefetch=1,
        grid=(n_batch // ns, n_groups),
        in_specs=[page_spec(g, k) for g in range(ns) for k in range(pg)]
        + [per_b(NSA_HEADS * LANE), per_b(2 * LANE), per_b(LANE),
           pl.BlockSpec((ns, LANE, state_t.shape[2]), lambda b, j, pt: (b, 0, 0)), per_b(LANE)]
        + [const(a) for a in consts],
        out_specs=per_b(NSA_HEADS * HEAD_DIM),
        scratch_shapes=[pltpu.VMEM((ns, n_cmp, 2 * LANE), F32), pltpu.VMEM((ns, n_pages + 1, LANE, PAGE_SIZE), BF16),
                        pltpu.VMEM((ns * pg, PAGE_SIZE, LANE), F32), pltpu.VMEM((ns, rows, LANE), BF16),
                        pltpu.VMEM((ns, rows, (n_pages + 1) * PAGE_SIZE), F32)],
    )
    return pl.pallas_call(
        functools.partial(_nsa_sample_kernel, n_seq=ns),
        grid_spec=grid_spec,
        out_shape=jax.ShapeDtypeStruct((n_batch * t_new, NSA_HEADS * HEAD_DIM), F32),
        compiler_params=_params(("parallel", "arbitrary")),
        name="nsa_sample",
    )(pt_l, *([pages_t] * (ns * pg)), q, nsarows, winrows, state_t, gates, *consts)


def _mla_sample_kernel(pt_ref, *refs, n_seq):
    del pt_ref
    pg = PAGES_PER_STEP
    pages = refs[:n_seq * pg]
    kss = refs[n_seq * pg:2 * n_seq * pg]
    (q_ref, new_ref, ksnew_ref, wukt_ref, wuvp_ref, o_ref, qa_ref, qr_ref, m_ref, l_ref, acc_ref) = refs[2 * n_seq * pg:]
    j = pl.program_id(1)
    n_groups = pl.num_programs(1)
    t_new = q_ref.shape[0] // n_seq
    rows = MLA_HEADS * t_new

    @pl.when(j == 0)
    def _():
        lane = lax.broadcasted_iota(jnp.int32, (t_new, LANE), 1)
        rope_lanes = (lane >= MLA_NOPE) & (lane < MLA_NOPE + MLA_ROPE)
        for g in range(n_seq):
            qblk = q_ref[g * t_new:(g + 1) * t_new, :]
            for h in range(MLA_HEADS):
                qh = qblk[:, h * LANE:(h + 1) * LANE]
                qa_ref[g, h * t_new:(h + 1) * t_new, :] = _dot(qh.astype(BF16), wukt_ref[h]).astype(BF16)
                qr = pltpu.roll(jnp.where(rope_lanes, qh, 0.0), LANE - MLA_NOPE, 1)
                qr_ref[g, h * t_new:(h + 1) * t_new, :] = qr[:, 0:MLA_ROPE].astype(BF16)
        m_ref[...] = jnp.full(m_ref.shape, NEG, F32)
        l_ref[...] = jnp.zeros(l_ref.shape, F32)
        acc_ref[...] = jnp.zeros(acc_ref.shape, F32)

    def update(g, s, pv):
        m_old = m_ref[g]
        m_new = jnp.maximum(m_old, jnp.max(s, axis=-1, keepdims=True))
        alpha = jnp.exp(m_old - m_new)
        p = jnp.exp(s - m_new)
        l_ref[g] = alpha * l_ref[g] + jnp.sum(p, axis=-1, keepdims=True)
        m_ref[g] = m_new
        yield
        acc_ref[g] = alpha * acc_ref[g] + pv(p.astype(BF16))

    def head_rows(ks):
        return jnp.concatenate([jnp.broadcast_to(ks[h:h + 1, :], (t_new, ks.shape[1])) for h in range(MLA_HEADS)],
                               axis=0)

    def step(g):
        ct = jnp.concatenate([pages[g * pg + k][...] for k in range(pg)], axis=1).astype(BF16)
        ks = jnp.concatenate([kss[g * pg + k][...] for k in range(pg)], axis=1)
        ct_lat = ct[0:KV_LORA, :]
        yield
        s = _dot(qa_ref[g], ct_lat) * head_rows(ks) + _dot(qr_ref[g], ct[KV_LORA:, :])
        yield
        yield from update(g, s, lambda p16: _dot_nt(p16, ct_lat))

    _round_robin([step(g) for g in range(n_seq)])

    def tail(g):
        tok = slice(g * t_new, (g + 1) * t_new)
        new16 = new_ref[tok, :].astype(BF16)
        new_lat = new16[:, 0:KV_LORA]
        r_t = lax.broadcasted_iota(jnp.int32, (rows, t_new), 0) % t_new
        c_t = lax.broadcasted_iota(jnp.int32, (rows, t_new), 1)
        s_new = (_dot_nt(qa_ref[g], new_lat) * head_rows(ksnew_ref[g]) + _dot_nt(qr_ref[g], new16[:, KV_LORA:]))
        yield
        yield from update(g, s_new + jnp.where(c_t <= r_t, 0.0, NEG), lambda p16: _dot(p16, new_lat))
        yield
        lat16 = (acc_ref[g] / l_ref[g]).astype(BF16)
        lat_wide = jnp.concatenate([lat16[h * t_new:(h + 1) * t_new, :] for h in range(MLA_HEADS)], axis=1)
        o_ref[tok, :] = _dot(lat_wide, wuvp_ref[...])

    @pl.when(j == n_groups - 1)
    def _():
        _round_robin([tail(g) for g in range(n_seq)])


def _mla_sample(pt_l, cache_pages, cache_ks_t, q, mlarows, ksnew_t, lw, n_batch, t_new, n_pages):
    pg = PAGES_PER_STEP
    ns = SAMPLE_SEQS if n_batch % SAMPLE_SEQS == 0 else 1
    n_groups = n_pages // pg
    rows = MLA_HEADS * t_new
    width = KV_LORA + MLA_ROPE

    def page_spec(g, k):
        return pl.BlockSpec((None, width, PAGE_SIZE), lambda b, j, pt: (pt[b * ns + g, j * pg + k], 0, 0))

    def ks_spec(g, k):
        return pl.BlockSpec((None, MLA_HEADS, PAGE_SIZE), lambda b, j, pt: (pt[b * ns + g, j * pg + k], 0, 0))

    per_b = lambda w: pl.BlockSpec((ns * t_new, w), lambda b, j, pt: (b, 0))
    const = lambda a: pl.BlockSpec(a.shape, lambda b, j, pt: (0,) * a.ndim, pipeline_mode=pl.Buffered(1))
    wuvp = lw["wuvp"].reshape(MLA_HEADS * KV_LORA, MLA_HEADS * MLA_V)
    grid_spec = pltpu.PrefetchScalarGridSpec(
        num_scalar_prefetch=1,
        grid=(n_batch // ns, n_groups),
        in_specs=[page_spec(g, k) for g in range(ns) for k in range(pg)]
        + [ks_spec(g, k) for g in range(ns) for k in range(pg)]
        + [per_b(MLA_HEADS * LANE), per_b(width), pl.BlockSpec((ns, MLA_HEADS, t_new), lambda b, j, pt: (b, 0, 0)),
           const(lw["wukt"]), const(wuvp)],
        out_specs=per_b(MLA_HEADS * MLA_V),
        scratch_shapes=[pltpu.VMEM((ns, rows, KV_LORA), BF16), pltpu.VMEM((ns, rows, MLA_ROPE), BF16),
                        pltpu.VMEM((ns, rows, 1), F32), pltpu.VMEM((ns, rows, 1), F32),
                        pltpu.VMEM((ns, rows, KV_LORA), F32)],
    )
    return pl.pallas_call(
        functools.partial(_mla_sample_kernel, n_seq=ns),
        grid_spec=grid_spec,
        out_shape=jax.ShapeDtypeStruct((n_batch * t_new, MLA_HEADS * MLA_V), F32),
        compiler_params=_params(("parallel", "arbitrary")),
        name="mla_sample",
    )(pt_l, *([cache_pages] * (ns * pg)), *([cache_ks_t] * (ns * pg)), q, mlarows, ksnew_t, lw["wukt"], wuvp)


def _swap_halves(a):
    half = a.shape[-1] // 2
    return jnp.concatenate([a[..., half:], a[..., :half]], axis=-1)


def _layer_weights(p, l):
    w_in = p["w_in"][l]
    zeros = lambda *s: jnp.zeros(s, F32)
    q_cols = jnp.concatenate([w_in[:, :OFF_NSA_KV].reshape(D_MODEL, NSA_HEADS, HEAD_DIM),
                              zeros(D_MODEL, NSA_HEADS, LANE - HEAD_DIM)], axis=-1).reshape(D_MODEL, -1)
    k_rope = w_in[:, OFF_KROPE:IN_COLS]
    e_cols = jnp.concatenate([w_in[:, OFF_GATE:OFF_QLAT], zeros(D_MODEL, HEAD_DIM - 3 * NSA_HEADS),
                              k_rope, _swap_halves(k_rope)], axis=-1)
    w1 = jnp.concatenate([q_cols, w_in[:, OFF_NSA_KV:OFF_GATE], w_in[:, OFF_QLAT:OFF_CKV],
                          w_in[:, OFF_CKV:OFF_KROPE], e_cols], axis=-1).astype(BF16)
    nn = p["nsa_norm"][l]
    ones64 = jnp.ones((HEAD_DIM,), F32)
    gq = jnp.concatenate([nn[0] * NSA_SCALE, zeros(HEAD_DIM)])[None]
    gkv = jnp.concatenate([jnp.stack([jnp.concatenate([nn[2], ones64]), jnp.concatenate([nn[3], ones64])]),
                           zeros(6, LANE)], axis=0)
    qu = p["w_q_up"][l].reshape(Q_LORA, MLA_HEADS, MLA_NOPE + MLA_ROPE)
    wqup = jnp.concatenate([qu, _swap_halves(qu[..., MLA_NOPE:])], axis=-1).reshape(Q_LORA, -1).astype(BF16)
    gn, gr = p["mla_nope_norm"][l], p["mla_rope_norm"][l]
    gmq = jnp.concatenate([gn[0] * gn[1], gr[0], _swap_halves(gr[0])])[None]
    ge = jnp.concatenate([zeros(HEAD_DIM), gr[1], _swap_halves(gr[1])])[None]
    wuk = jnp.concatenate([p["w_uk"][l], zeros(KV_LORA, MLA_HEADS, LANE - MLA_NOPE)], axis=-1)
    wuk = wuk.reshape(KV_LORA, -1).astype(BF16)
    wuv = p["w_uv"][l].reshape(KV_LORA, -1).astype(BF16)
    wukt = jnp.concatenate([jnp.transpose(p["w_uk"][l], (1, 2, 0)), zeros(MLA_HEADS, LANE - MLA_NOPE, KV_LORA)],
                           axis=1).astype(BF16)
    eye = jnp.eye(MLA_HEADS, dtype=F32)
    wuvp = (jnp.transpose(p["w_uv"][l], (1, 0, 2))[:, :, None, :] * eye[:, None, :, None]).reshape(
        MLA_HEADS, KV_LORA, MLA_HEADS * MLA_V).astype(BF16)
    w1r = p["cmp_w1"][l].reshape(2, 2, CMP_STRIDE, HEAD_DIM, CMP_HIDDEN)
    z = zeros(CMP_STRIDE, HEAD_DIM, CMP_HIDDEN)
    k_rows = jnp.concatenate([w1r[0, 0], z, w1r[0, 1], z], axis=-1)
    v_rows = jnp.concatenate([z, w1r[1, 0], z, w1r[1, 1]], axis=-1)
    cw1 = jnp.concatenate([k_rows, v_rows], axis=1).reshape(CMP_STRIDE * LANE, 2 * LANE).astype(BF16)
    pe = p["cmp_pe"][l].reshape(2, 2, CMP_STRIDE, HEAD_DIM)
    pe_rows = jnp.concatenate([pe[0], pe[1]], axis=-1).reshape(2, CMP_STRIDE * LANE)
    cpe = jnp.concatenate([pe_rows, zeros(6, CMP_STRIDE * LANE)], axis=0).astype(BF16)
    w2 = p["cmp_w2"][l]
    z2 = zeros(CMP_HIDDEN, HEAD_DIM)
    cw2 = jnp.concatenate([jnp.concatenate([w2[0], z2], axis=1), jnp.concatenate([z2, w2[1]], axis=1)],
                          axis=0).astype(BF16)
    gkc = jnp.concatenate([nn[1], ones64])[None]
    wup = jnp.transpose(p["w_up"][l].reshape(D_MODEL, 2 * N_FF_CHUNKS, FF_CHUNK), (1, 0, 2)).astype(BF16)
    cwb = jnp.concatenate([p["conv_w"][l], p["conv_b"][l][None], zeros(8 - CONV_W - 1, 2 * D_FF)], axis=0)
    cw = jnp.transpose(cwb.reshape(8, 2 * N_FF_CHUNKS, FF_CHUNK), (1, 0, 2))
    wdn = p["w_down"][l].reshape(N_FF_CHUNKS, FF_CHUNK, D_MODEL).astype(BF16)
    return dict(gattn=p["attn_norm"][l][None], w1=w1, gq=gq, gkv=gkv, gql=p["mla_q_lat_norm"][l][None],
                wqup=wqup, gmq=gmq, gc=p["mla_kv_norm"][l][None], wuk=wuk, wuv=wuv, ge=ge, wukt=wukt, wuvp=wuvp,
                cw1=cw1, cpe=cpe, cw2=cw2, gkc=gkc, wo=p["w_o"][l].astype(BF16), gffn=p["ffn_norm"][l][None],
                wup=wup, cw=cw, wdn=wdn)


def _rope_table(pos):
    half = MLA_ROPE // 2
    inv_freq = ROPE_THETA ** (-jnp.arange(half, dtype=F32) / half)
    ang = pos.astype(F32)[:, None] * inv_freq[None, :]
    cos, sin = jnp.cos(ang), jnp.sin(ang)
    n = pos.shape[0]
    z = lambda w: jnp.zeros((n, w), F32)
    cc = jnp.concatenate([cos, cos], axis=1)
    ss = jnp.concatenate([-sin, sin], axis=1)
    ctab = jnp.concatenate([jnp.full((n, MLA_NOPE), MLA_SCALE, F32), cc * MLA_SCALE, z(MLA_ROPE)], axis=1)
    stab = jnp.concatenate([z(MLA_NOPE + MLA_ROPE), ss * MLA_SCALE], axis=1)
    ck = jnp.concatenate([z(MLA_NOPE), cc, z(MLA_ROPE)], axis=1)
    sk = jnp.concatenate([z(MLA_NOPE + MLA_ROPE), ss], axis=1)
    return jnp.concatenate([ctab, stab, ck, sk], axis=1)


def _cmp_map(n_rows, n_sel, n_lanes):
    c0 = np.arange(n_rows)[:, None] * CMP_STRIDE
    s0 = np.arange(n_lanes)[None, :] * SEL_BLOCK
    m = (c0 < s0 + SEL_BLOCK) & (c0 + CMP_BLOCK > s0) & (np.arange(n_lanes)[None, :] < n_sel)
    return jnp.asarray(m.astype(np.float32), BF16)


def _prompt_tables(rel_bias, seq):
    t = np.arange(Q_BLOCK)[:, None]
    j = np.arange(LANE)[None, :]
    n_var = -(-(_THR[-1] + LANE) // LANE)
    buckets = [_bucket_np(d * LANE + t - j) for d in range(n_var)] + [np.full((Q_BLOCK, LANE), REL_BUCKETS - 1)]
    tbsel = _bias_tables(rel_bias, np.stack(buckets))
    jw = np.arange(WINDOW + Q_BLOCK)[None, :]
    dist_w = WINDOW + t - jw
    tbwin = _bias_tables(rel_bias, _masked_bucket(dist_w, (dist_w >= 0) & (dist_w < WINDOW))[None])[0]
    n_cmp = seq // CMP_STRIDE
    cmap = _cmp_map(n_cmp, seq // SEL_BLOCK, LANE)
    far = _THR[-1] + CMP_STRIDE * (LANE - 1) + CMP_BLOCK - 1
    n_varc = -(-far // Q_BLOCK)
    buckets = [_bucket_np(k * Q_BLOCK + t - (j * CMP_STRIDE + CMP_BLOCK - 1)) for k in range(n_varc)]
    buckets.append(np.full((Q_BLOCK, LANE), REL_BUCKETS - 1))
    tbcmp = _bias_tables(rel_bias, np.stack(buckets))
    return tbcmp, tbsel, tbwin, cmap


def _sample_tables(rel_bias, past, t_new, wb):
    t = np.arange(t_new)[:, None]
    n_cmp = past // CMP_STRIDE
    n = np.arange(n_cmp)[None, :]
    dist_c = past + t - (n * CMP_STRIDE + CMP_BLOCK - 1)
    ok_c = (dist_c >= 0) & (n < n_cmp - 1)
    tbc = _bias_tables(rel_bias, _masked_bucket(dist_c, ok_c)[None])[0]
    k = np.arange(past + PAGE_SIZE)[None, :]
    dist_s = past + t - k
    ok_s = (dist_s >= 0) & (k < past + t_new)
    tbs = _bias_tables(rel_bias, _masked_bucket(dist_s, ok_s)[None])[0]
    i = np.arange(wb + PAGE_SIZE)[None, :]
    dist_w = wb + t - i
    ok_w = (dist_w >= 0) & (dist_w < WINDOW) & (i < wb + t_new)
    tbw = _bias_tables(rel_bias, _masked_bucket(dist_w, ok_w)[None])[0]
    n_sel = past // SEL_BLOCK + 1
    n_lanes = -(-n_sel // LANE) * LANE
    place = np.zeros((NSA_HEADS, LANE, NSA_HEADS * HEAD_DIM), np.float32)
    for h in range(NSA_HEADS):
        place[h, HEAD_DIM + np.arange(HEAD_DIM), h * HEAD_DIM + np.arange(HEAD_DIM)] = 1.0
    expand = (np.arange(n_lanes)[:, None] == (np.arange(past + PAGE_SIZE)[None, :] // SEL_BLOCK)).astype(np.float32)
    return dict(tbc_s=tbc, tbs_s=tbs, tbw_s=tbw, map_s=_cmp_map(n_cmp, n_sel, n_lanes),
                place=jnp.asarray(place.reshape(NSA_HEADS * LANE, NSA_HEADS * HEAD_DIM), BF16),
                expand_s=jnp.asarray(expand, BF16))


def kernel(x_prompt, x_sample, cache_nsa, cache_mla, cache_mla_kscale, state_win, state_conv, page_table,
           rel_bias, attn_norm, w_in, nsa_norm, cmp_pe, cmp_w1, cmp_w2, mla_q_lat_norm, mla_kv_norm,
           w_q_up, w_uk, w_uv, mla_nope_norm, mla_rope_norm, w_o, ffn_norm, w_up, conv_w, conv_b, w_down):
    p = dict(attn_norm=attn_norm, w_in=w_in, nsa_norm=nsa_norm, cmp_pe=cmp_pe, cmp_w1=cmp_w1, cmp_w2=cmp_w2,
             mla_q_lat_norm=mla_q_lat_norm, mla_kv_norm=mla_kv_norm, w_q_up=w_q_up, w_uk=w_uk, w_uv=w_uv,
             mla_nope_norm=mla_nope_norm, mla_rope_norm=mla_rope_norm, w_o=w_o, ffn_norm=ffn_norm, w_up=w_up,
             conv_w=conv_w, conv_b=conv_b, w_down=w_down)
    depth = w_in.shape[0]
    nb, seq, _ = x_prompt.shape
    db, t_new, _ = x_sample.shape
    n_pool = cache_nsa.shape[1]
    n_pages = page_table.shape[1]
    past = n_pages * PAGE_SIZE
    wb = state_win.shape[2]
    assert seq % MLA_TQ == 0 and n_pages % PAGES_PER_STEP == 0 and past % (8 * PAGE_SIZE) == 0
    assert wb == WINDOW and t_new == 8 and seq >= WINDOW

    tm_p = 256
    tm_f = 512 if seq % 512 == 0 else 256
    ts = db * t_new
    tm_s = 128 if ts % 128 == 0 else ts
    rope_p = _rope_table(jnp.arange(seq, dtype=jnp.int32))
    rope_s = jnp.tile(_rope_table(past + jnp.arange(t_new, dtype=jnp.int32)), (db, 1))
    tbcmp, tbsel, tbwin, cmap = _prompt_tables(rel_bias, seq)
    tabs = _sample_tables(rel_bias, past, t_new, wb)
    nsa_pages_t = jnp.transpose(cache_nsa, (0, 1, 3, 4, 5, 2)).reshape(depth * n_pool, 4 * HEAD_DIM, PAGE_SIZE)
    mla_pages_t = jnp.transpose(cache_mla, (0, 1, 3, 2)).reshape(depth * n_pool, KV_LORA + MLA_ROPE, PAGE_SIZE)
    ks_pages_t = jnp.transpose(cache_mla_kscale, (0, 1, 3, 2)).reshape(depth * n_pool, MLA_HEADS, PAGE_SIZE)
    state_t = jnp.transpose(state_win, (0, 1, 3, 4, 5, 2)).reshape(depth, db, LANE, wb)

    xp = x_prompt.reshape(nb * seq, D_MODEL)
    xs = x_sample.reshape(ts, D_MODEL)
    outs_p = [[] for _ in range(5)]
    outs_s = [[] for _ in range(5)]
    for l in range(depth):
        lw = _layer_weights(p, l)
        (q_nsa, nsarows, winrows, cmp16, sel16, win16, gates, q_mla, mlarows, kscale, kfull, v16) = _proj(
            xp, lw, rope_p, tm_p, BF16)
        kvc = _compress(cmp16.reshape(nb * seq // CMP_STRIDE, CMP_STRIDE * LANE), lw, nb)
        win16p = jnp.pad(win16.reshape(nb, seq, LANE), ((0, 0), (WINDOW, 0), (0, 0)))
        mix_nsa = _nsa_prompt(q_nsa, kvc, sel16, win16p, gates, tbcmp, tbsel, tbwin, cmap, nb, seq)
        mix_mla = _mla_prompt(q_mla, kfull, v16, nb, seq, MLA_TQ)
        xp, tail = _ffn(xp, mix_nsa, mix_mla, lw, tm_f, tiles_per_seq=seq // tm_f)
        tiles = seq // tm_f
        last = tail.reshape(nb, tiles, 2 * N_FF_CHUNKS, 8, FF_CHUNK)[:, tiles - 1, :, 8 - (CONV_W - 1):, :]
        outs_p[0].append(nsarows.reshape(nb, seq, 4, 1, HEAD_DIM))
        outs_p[1].append(mlarows.reshape(nb, seq, KV_LORA + MLA_ROPE))
        outs_p[2].append(kscale[:, :MLA_HEADS].reshape(nb, seq, MLA_HEADS))
        outs_p[3].append(winrows.reshape(nb, seq, 2, 1, HEAD_DIM)[:, seq - min(WINDOW, seq):])
        outs_p[4].append(jnp.transpose(last, (0, 2, 1, 3)).reshape(nb, CONV_W - 1, 2 * D_FF))
        (q_nsa, nsarows, winrows, _, _, _, gates, q_mla, mlarows, kscale, _, _) = _proj(xs, lw, rope_s, ts, F32)
        pt_l = page_table + l * n_pool
        mix_nsa = _nsa_sample(pt_l, nsa_pages_t, q_nsa, nsarows, winrows, state_t[l], gates, lw, tabs, db, t_new,
                              n_pages)
        ksnew_t = jnp.swapaxes(kscale[:, :MLA_HEADS].reshape(db, t_new, MLA_HEADS), 1, 2)
        mix_mla = _mla_sample(pt_l, mla_pages_t, ks_pages_t, q_mla, mlarows, ksnew_t, lw, db, t_new, n_pages)
        sc = state_conv[l]
        zpad = jnp.zeros((db, t_new - 1, 2 * D_FF), F32)
        a1 = jnp.concatenate([sc[:, 1:2], zpad], axis=1)
        a2 = jnp.concatenate([sc, zpad[:, 1:]], axis=1)
        chunked = lambda a: jnp.transpose(a.reshape(ts, 2 * N_FF_CHUNKS, FF_CHUNK), (1, 0, 2))
        xs, tail = _ffn(xs, mix_nsa.astype(BF16), mix_mla.astype(BF16), lw, tm_s, period=t_new,
                        a1=chunked(a1), a2=chunked(a2))
        h_rows = jnp.transpose(tail, (0, 2, 1, 3)).reshape(db, t_new, 2 * D_FF)
        outs_s[0].append(nsarows.reshape(db, t_new, 4, 1, HEAD_DIM))
        outs_s[1].append(mlarows.reshape(db, t_new, KV_LORA + MLA_ROPE))
        outs_s[2].append(kscale[:, :MLA_HEADS].reshape(db, t_new, MLA_HEADS))
        outs_s[3].append(jnp.concatenate([state_win[l][:, t_new:], winrows.reshape(db, t_new, 2, 1, HEAD_DIM)], axis=1))
        outs_s[4].append(h_rows[:, t_new - (CONV_W - 1):])
    stack = lambda o: jnp.stack(o, axis=0)
    nsa_p, mla_p, ks_p, win_p, cv_p = [stack(o) for o in outs_p]
    nsa_s, mla_s, ks_s, win_s, cv_s = [stack(o) for o in outs_s]
    return (xp.reshape(nb, seq, D_MODEL), xs.reshape(db, t_new, D_MODEL), nsa_p, mla_p, ks_p, win_p, cv_p,
            nsa_s, mla_s, ks_s, win_s, cv_s)
```
